```python
import jax, jax.numpy as jnp
from jax import lax
import numpy as np

D_MODEL = 4096
BATCH = 4
SEQ = 4096
DEPTH = 1

D_LRU = D_MODEL // 2
N_LRU_BLOCKS = 16
LRU_BLOCK = D_LRU // N_LRU_BLOCKS
LRU_C = 8.0
CONV_LRU = 4
HEAD_DIM = 128
N_HEADS = (D_MODEL // 2) // HEAD_DIM
N_KV_HEADS = 4
D_ATTN = N_HEADS * HEAD_DIM
D_KV = N_KV_HEADS * HEAD_DIM
D_MIX = D_LRU + D_ATTN
N_IDX_HEADS = 32
IDX_DIM = 64
TOPK_MAX = 256
Q_BLOCK = 128
D_FF = 11008
CONV_FFN = 3
RMS_EPS = 1e-6
N_MOD = 6

SZ_LRU_X = D_LRU
SZ_LRU_GATE = D_LRU
SZ_Q = D_ATTN
SZ_K = D_KV
SZ_V = D_KV
SZ_QIDX = N_IDX_HEADS * IDX_DIM
SZ_KIDX = IDX_DIM
SZ_WIDX = N_IDX_HEADS
D_IN = SZ_LRU_X + SZ_LRU_GATE + SZ_Q + SZ_K + SZ_V + SZ_QIDX + SZ_KIDX + SZ_WIDX
OFF_1 = SZ_LRU_X
OFF_2 = OFF_1 + SZ_LRU_GATE
OFF_3 = OFF_2 + SZ_Q
OFF_4 = OFF_3 + SZ_K
OFF_5 = OFF_4 + SZ_V
OFF_6 = OFF_5 + SZ_QIDX
OFF_7 = OFF_6 + SZ_KIDX

kernel_name = "hymba_rglru_dsa_convffn_adaln"


def rmsnorm(x, g):
    xf = x.astype(jnp.float32)
    y = xf * lax.rsqrt(jnp.mean(xf * xf, axis=-1, keepdims=True) + RMS_EPS)
    return (y * g.astype(jnp.float32)).astype(x.dtype)


def causal_dwconv(x, w, b):
    k_w = w.shape[0]
    s = x.shape[1]
    xp = jnp.pad(x, ((0, 0), (k_w - 1, 0), (0, 0)))
    y = b
    for j in range(k_w):
        y = y + w[j] * xp[:, j:j + s]
    return y


def rg_lru(x, w_a, b_a, w_x, b_x, lam):
    f32 = jnp.float32
    bsz, s, _ = x.shape
    xf = x.astype(f32)
    xb = xf.reshape(bsz, s, N_LRU_BLOCKS, LRU_BLOCK)
    r = jax.nn.sigmoid(jnp.einsum('bsni,nij->bsnj', xb, w_a.astype(f32)).reshape(bsz, s, D_LRU) + b_a.astype(f32))
    i = jax.nn.sigmoid(jnp.einsum('bsni,nij->bsnj', xb, w_x.astype(f32)).reshape(bsz, s, D_LRU) + b_x.astype(f32))
    log_a = -LRU_C * jax.nn.softplus(-lam.astype(f32)) * r
    a = jnp.exp(log_a)
    u = jnp.sqrt(-jnp.expm1(2.0 * log_a)) * (i * xf)

    def combine(e1, e2):
        a1, b1 = e1
        a2, b2 = e2
        return a1 * a2, a2 * b1 + b2

    _, h = lax.associative_scan(combine, (a, u), axis=1)
    return h.astype(x.dtype)


def dsa_attention(q, k, v, q_idx, k_idx, w_idx):
    f32 = jnp.float32
    bsz, s = q.shape[0], q.shape[1]
    n_blk = s // Q_BLOCK
    k_sel = min(TOPK_MAX, s // 4)
    rep = N_HEADS // N_KV_HEADS
    scale = HEAD_DIM ** -0.5
    k_idx_f = k_idx.astype(f32)
    key_pos = jnp.arange(s)

    def to_blocks(t):
        return jnp.swapaxes(t.reshape((bsz, n_blk, Q_BLOCK) + t.shape[2:]), 0, 1)

    def gather_rows(kb, ib):
        return kb[ib]

    def block(args):
        qb, qib, wib, t0 = args
        qpos = t0 + jnp.arange(Q_BLOCK)
        causal = key_pos[None, :] <= qpos[:, None]
        dots = jnp.einsum('bqhd,bsd->bqhs', qib.astype(f32), k_idx_f)
        score = jnp.einsum('bqhs,bqh->bqs', jax.nn.relu(dots), wib.astype(f32))
        score = jnp.where(causal[None], score, -jnp.inf)
        _, sel = lax.top_k(score, k_sel)
        valid = sel <= qpos[None, :, None]
        kg = jax.vmap(gather_rows)(k, sel)
        vg = jax.vmap(gather_rows)(v, sel)
        qg = qb.reshape(bsz, Q_BLOCK, N_KV_HEADS, rep, HEAD_DIM).astype(f32)
        logits = jnp.einsum('bqgrd,bqkgd->bqgrk', qg, kg.astype(f32)) * scale
        logits = jnp.where(valid[:, :, None, None, :], logits, -jnp.inf)
        p = jax.nn.softmax(logits, axis=-1)
        o = jnp.einsum('bqgrk,bqkgd->bqgrd', p, vg.astype(f32))
        return o.reshape(bsz, Q_BLOCK, D_ATTN).astype(q.dtype)

    t0s = jnp.arange(n_blk) * Q_BLOCK
    out = lax.map(block, (to_blocks(q), to_blocks(q_idx), to_blocks(w_idx), t0s))
    return jnp.swapaxes(out, 0, 1).reshape(bsz, s, D_ATTN)


def hybrid_layer(x, c, w_ada, b_ada, g_pre_mix, g_post_mix, g_pre_ffn, g_post_ffn, w_in,
                 conv_lru_w, conv_lru_b, w_rg_a, b_rg_a, w_rg_x, b_rg_x, lru_lambda,
                 g_grp_lru, g_grp_attn, w_out, w_ffn_in, conv_ffn_w, conv_ffn_b, w_ffn_out):
    bsz, s, _ = x.shape
    mod = (jax.nn.silu(c) @ w_ada + b_ada)[:, None, :]
    sh1, sc1, gt1, sh2, sc2, gt2 = jnp.split(mod, N_MOD, axis=-1)

    h = rmsnorm(x, g_pre_mix) * (1.0 + sc1) + sh1
    proj = h @ w_in
    x_lru, gate_lru, q, k, v, q_idx, k_idx, w_idx = jnp.split(
        proj, [OFF_1, OFF_2, OFF_3, OFF_4, OFF_5, OFF_6, OFF_7], axis=-1)

    x_lru = causal_dwconv(x_lru, conv_lru_w, conv_lru_b)
    y_lru = rg_lru(x_lru, w_rg_a, b_rg_a, w_rg_x, b_rg_x, lru_lambda) * jax.nn.gelu(gate_lru)

    y_attn = dsa_attention(q.reshape(bsz, s, N_HEADS, HEAD_DIM),
                           k.reshape(bsz, s, N_KV_HEADS, HEAD_DIM),
                           v.reshape(bsz, s, N_KV_HEADS, HEAD_DIM),
                           q_idx.reshape(bsz, s, N_IDX_HEADS, IDX_DIM), k_idx, w_idx)

    merged = jnp.concatenate([rmsnorm(y_lru, g_grp_lru), rmsnorm(y_attn, g_grp_attn)], axis=-1)
    mix = merged @ w_out
    x = x + gt1 * rmsnorm(mix, g_post_mix)

    h = rmsnorm(x, g_pre_ffn) * (1.0 + sc2) + sh2
    u = causal_dwconv(h @ w_ffn_in, conv_ffn_w, conv_ffn_b)
    u_gate, u_up = jnp.split(u, 2, axis=-1)
    f = (jax.nn.gelu(u_gate) * u_up) @ w_ffn_out
    return x + gt2 * rmsnorm(f, g_post_ffn)


def setup_inputs(seed: int = 0) -> dict:
    key = jax.random.key(seed)
    ks = jax.random.split(key, 24)
    f32 = jnp.float32
    nrm = lambda k, shape, s: jax.random.normal(k, shape, f32) * s
    L = DEPTH
    u = jax.random.uniform(ks[15], (L, D_LRU), f32, 0.9, 0.999)
    a0 = u ** (1.0 / LRU_C)
    lru_lambda = jnp.log(a0) - jnp.log1p(-a0)
    return {
        'x': nrm(ks[0], (BATCH, SEQ, D_MODEL), 1.0),
        'c': nrm(ks[1], (BATCH, D_MODEL), 1.0),
        'w_ada': nrm(ks[2], (L, D_MODEL, N_MOD * D_MODEL), 0.5 * D_MODEL ** -0.5),
        'b_ada': nrm(ks[3], (L, N_MOD * D_MODEL), 0.02),
        'g_pre_mix': 1.0 + nrm(ks[4], (L, D_MODEL), 0.02),
        'g_post_mix': 1.0 + nrm(ks[5], (L, D_MODEL), 0.02),
        'g_pre_ffn': 1.0 + nrm(ks[6], (L, D_MODEL), 0.02),
        'g_post_ffn': 1.0 + nrm(ks[7], (L, D_MODEL), 0.02),
        'w_in': nrm(ks[8], (L, D_MODEL, D_IN), D_MODEL ** -0.5),
        'conv_lru_w': nrm(ks[9], (L, CONV_LRU, D_LRU), CONV_LRU ** -0.5),
        'conv_lru_b': nrm(ks[10], (L, D_LRU), 0.02),
        'w_rg_a': nrm(ks[11], (L, N_LRU_BLOCKS, LRU_BLOCK, LRU_BLOCK), LRU_BLOCK ** -0.5),
        'b_rg_a': nrm(ks[12], (L, D_LRU), 0.02),
        'w_rg_x': nrm(ks[13], (L, N_LRU_BLOCKS, LRU_BLOCK, LRU_BLOCK), LRU_BLOCK ** -0.5),
        'b_rg_x': nrm(ks[14], (L, D_LRU), 0.02),
        'lru_lambda': lru_lambda,
        'g_grp_lru': 1.0 + nrm(ks[16], (L, D_LRU), 0.02),
        'g_grp_attn': 1.0 + nrm(ks[17], (L, D_ATTN), 0.02),
        'w_out': nrm(ks[18], (L, D_MIX, D_MODEL), D_MIX ** -0.5),
        'w_ffn_in': nrm(ks[19], (L, D_MODEL, 2 * D_FF), D_MODEL ** -0.5),
        'conv_ffn_w': nrm(ks[20], (L, CONV_FFN, 2 * D_FF), CONV_FFN ** -0.5),
        'conv_ffn_b': nrm(ks[21], (L, 2 * D_FF), 0.02),
        'w_ffn_out': nrm(ks[22], (L, D_FF, D_MODEL), D_FF ** -0.5),
    }


def reference(x, c, w_ada, b_ada, g_pre_mix, g_post_mix, g_pre_ffn, g_post_ffn, w_in,
              conv_lru_w, conv_lru_b, w_rg_a, b_rg_a, w_rg_x, b_rg_x, lru_lambda,
              g_grp_lru, g_grp_attn, w_out, w_ffn_in, conv_ffn_w, conv_ffn_b, w_ffn_out):
    for l in range(DEPTH):
        x = hybrid_layer(x, c, w_ada[l], b_ada[l], g_pre_mix[l], g_post_mix[l], g_pre_ffn[l],
                         g_post_ffn[l], w_in[l], conv_lru_w[l], conv_lru_b[l], w_rg_a[l], b_rg_a[l],
                         w_rg_x[l], b_rg_x[l], lru_lambda[l], g_grp_lru[l], g_grp_attn[l], w_out[l],
                         w_ffn_in[l], conv_ffn_w[l], conv_ffn_b[l], w_ffn_out[l])
    return x
```

```python
import functools

import jax
import jax.numpy as jnp
from jax import lax
from jax.experimental import pallas as pl
from jax.experimental.pallas import tpu as pltpu

F32 = jnp.float32
BF16 = jnp.bfloat16
I32 = jnp.int32

RMS_EPS = 1e-6
LRU_C = 8.0
LRU_BLOCK = 128
HEAD_DIM = 128
N_KV_HEADS = 4
N_IDX_HEADS = 32
IDX_DIM = 64
TOPK_MAX = 256

LANES = 128
SUBLANES = 8
VMEM_LIMIT = 56 * 1024 * 1024

INT_MIN = -(2 ** 31)
NEG_BIG = -1e30


def _cparams(sem):
    return pltpu.CompilerParams(dimension_semantics=sem, vmem_limit_bytes=VMEM_LIMIT)


def _rms_scale(v):
    return lax.rsqrt(jnp.mean(v * v, axis=-1, keepdims=True) + RMS_EPS)


def _gelu_tanh(v):
    return 0.5 * v * (1.0 + jnp.tanh(0.7978845608028654 * (v + 0.044715 * (v * v * v))))


def _shift_rows(cur, prev8, d):
    rolled = pltpu.roll(cur, d, 0)
    head = pltpu.roll(prev8, d, 0)
    row = lax.broadcasted_iota(I32, (SUBLANES, cur.shape[1]), 0)
    top = jnp.where(row < d, head, rolled[0:SUBLANES])
    return jnp.concatenate([top, rolled[SUBLANES:]], axis=0)


def _mod_kernel(c_ref, w_ref, b_ref, o_ref):
    c = c_ref[...]
    s = c * jax.nn.sigmoid(c)
    o_ref[...] = jnp.dot(s, w_ref[...], preferred_element_type=F32,
                         precision=lax.Precision.HIGHEST) + b_ref[...]


def _modulation(c, w_ada, b_ada, tn=512):
    bsz, d = c.shape
    n = w_ada.shape[1]
    rows = -(-bsz // SUBLANES) * SUBLANES
    c_pad = jnp.zeros((rows, d), F32).at[:bsz].set(c)
    out = pl.pallas_call(
        _mod_kernel,
        out_shape=jax.ShapeDtypeStruct((rows, n), F32),
        grid=(n // tn,),
        in_specs=[pl.BlockSpec((rows, d), lambda j: (0, 0)),
                  pl.BlockSpec((d, tn), lambda j: (0, j)),
                  pl.BlockSpec((1, tn), lambda j: (0, j))],
        out_specs=pl.BlockSpec((rows, tn), lambda j: (0, j)),
        compiler_params=_cparams(("arbitrary",)),
        name="modulation",
    )(c_pad, w_ada, b_ada.reshape(1, n))
    return out[:bsz]


def _inproj_kernel(x_ref, g_ref, sc_ref, sh_ref, w_ref, ws_ref, o_ref, os_ref, h_ref):
    @pl.when(pl.program_id(2) == 0)
    def _():
        x = x_ref[...]
        y = x * _rms_scale(x) * g_ref[...]
        h = (y * (1.0 + sc_ref[...]) + sh_ref[...]).astype(BF16)
        h_ref[...] = h
        os_ref[...] = jnp.dot(h, ws_ref[...], preferred_element_type=F32)

    o_ref[...] = jnp.dot(h_ref[...], w_ref[...], preferred_element_type=F32).astype(BF16)


def _in_proj(x, g, sc, sh, w_main, w_small, bm, bn):
    bsz, s, d = x.shape
    n = w_main.shape[1]
    ns = w_small.shape[1]
    return pl.pallas_call(
        _inproj_kernel,
        out_shape=(jax.ShapeDtypeStruct((bsz, s, n), BF16), jax.ShapeDtypeStruct((bsz, s, ns), F32)),
        grid=(bsz, s // bm, n // bn),
        in_specs=[pl.BlockSpec((None, bm, d), lambda b, i, j: (b, i, 0)),
                  pl.BlockSpec((1, d), lambda b, i, j: (0, 0)),
                  pl.BlockSpec((None, 1, d), lambda b, i, j: (b, 0, 0)),
                  pl.BlockSpec((None, 1, d), lambda b, i, j: (b, 0, 0)),
                  pl.BlockSpec((d, bn), lambda b, i, j: (0, j)),
                  pl.BlockSpec((d, ns), lambda b, i, j: (0, 0))],
        out_specs=(pl.BlockSpec((None, bm, bn), lambda b, i, j: (b, i, j)),
                   pl.BlockSpec((None, bm, ns), lambda b, i, j: (b, i, 0))),
        scratch_shapes=[pltpu.VMEM((bm, d), BF16)],
        compiler_params=_cparams(("arbitrary", "arbitrary", "arbitrary")),
        name="in_proj",
    )(x, g, sc, sh, w_main, w_small)


def _rglru_kernel(xl_ref, gate_ref, cw_ref, cb_ref, wa_ref, ba_ref, wx_ref, bx_ref, lam_ref, g_ref,
                  o_ref, halo_ref, hc_ref, a_scr, b_scr):
    bt, dl = xl_ref.shape
    nblk = dl // LRU_BLOCK

    @pl.when(pl.program_id(1) == 0)
    def _():
        halo_ref[...] = jnp.zeros_like(halo_ref)
        hc_ref[...] = jnp.zeros_like(hc_ref)

    xl = xl_ref[...].astype(F32)
    prev = halo_ref[...]
    halo_ref[...] = xl[bt - SUBLANES:, :]

    cw = cw_ref[...]
    kw = cw.shape[0]
    xc = cb_ref[...] + cw[kw - 1:kw, :] * xl
    for dlt in range(1, kw):
        xc = xc + cw[kw - 1 - dlt:kw - dlt, :] * _shift_rows(xl, prev, dlt)

    xcb = xc.astype(BF16)
    r_parts, i_parts = [], []
    for n in range(nblk):
        blk = xcb[:, n * LRU_BLOCK:(n + 1) * LRU_BLOCK]
        r_parts.append(jnp.dot(blk, wa_ref[n], preferred_element_type=F32))
        i_parts.append(jnp.dot(blk, wx_ref[n], preferred_element_type=F32))
    r = jax.nn.sigmoid(jnp.concatenate(r_parts, axis=1) + ba_ref[...])
    ig = jax.nn.sigmoid(jnp.concatenate(i_parts, axis=1) + bx_ref[...])

    z = -lam_ref[...]
    softplus = jnp.maximum(z, 0.0) + jnp.log(1.0 + jnp.exp(-jnp.abs(z)))
    log_a = (-LRU_C * softplus) * r
    a = jnp.exp(log_a)
    u = jnp.sqrt(1.0 - jnp.exp(2.0 * log_a)) * (ig * xc)

    rowmod = lax.broadcasted_iota(I32, (bt, dl), 0) % SUBLANES
    b = u
    for sft in (1, 2, 4):
        keep = rowmod >= sft
        a_prev = jnp.where(keep, pltpu.roll(a, sft, 0), 1.0)
        b_prev = jnp.where(keep, pltpu.roll(b, sft, 0), 0.0)
        b = a * b_prev + b
        a = a * a_prev
    a_scr[...] = a
    b_scr[...] = b

    def group(gi, carry):
        rows = pl.ds(pl.multiple_of(gi * SUBLANES, SUBLANES), SUBLANES)
        h = a_scr[rows, :] * carry + b_scr[rows, :]
        b_scr[rows, :] = h
        return jnp.broadcast_to(h[SUBLANES - 1:SUBLANES, :], (SUBLANES, dl))

    hc_ref[...] = lax.fori_loop(0, bt // SUBLANES, group, hc_ref[...])

    y = b_scr[...] * _gelu_tanh(gate_ref[...].astype(F32))
    o_ref[...] = (y * _rms_scale(y) * g_ref[...]).astype(BF16)


def _rglru(proj, x_off, cw, cb, wa, ba, wx, bx, lam, g, bt):
    bsz, s, _ = proj.shape
    dl = cw.shape[1]
    nblk = dl // LRU_BLOCK
    x_blk = x_off // dl
    vec = lambda: pl.BlockSpec((1, dl), lambda b, t: (0, 0))
    return pl.pallas_call(
        _rglru_kernel,
        out_shape=jax.ShapeDtypeStruct((bsz, s, dl), BF16),
        grid=(bsz, s // bt),
        in_specs=[pl.BlockSpec((None, bt, dl), lambda b, t: (b, t, x_blk)),
                  pl.BlockSpec((None, bt, dl), lambda b, t: (b, t, x_blk + 1)),
                  pl.BlockSpec((cw.shape[0], dl), lambda b, t: (0, 0)),
                  vec(),
                  pl.BlockSpec((nblk, LRU_BLOCK, LRU_BLOCK), lambda b, t: (0, 0, 0)),
                  vec(),
                  pl.BlockSpec((nblk, LRU_BLOCK, LRU_BLOCK), lambda b, t: (0, 0, 0)),
                  vec(), vec(), vec()],
        out_specs=pl.BlockSpec((None, bt, dl), lambda b, t: (b, t, 0)),
        scratch_shapes=[pltpu.VMEM((SUBLANES, dl), F32), pltpu.VMEM((SUBLANES, dl), F32),
                        pltpu.VMEM((bt, dl), F32), pltpu.VMEM((bt, dl), F32)],
        compiler_params=_cparams(("arbitrary", "arbitrary")),
        name="rglru",
    )(proj, proj, cw, cb, wa, ba, wx, bx, lam, g)


def _dsa_kernel(q_ref, qi_ref, k_ref, v_ref, kd_ref, w_ref, g_ref, o_ref,
                key_scr, m_scr, l_scr, acc_scr, out_scr, *, k_sel, rep):
    bq = q_ref.shape[0]
    nkb_all, _, kb2 = kd_ref.shape
    kb_sz = kb2 // 2
    n_groups = q_ref.shape[1] // (rep * HEAD_DIM)
    qblk = pl.program_id(1)
    nkb = lax.div((qblk + 1) * bq + kb_sz - 1, kb_sz)
    n_pairs = qi_ref.shape[1] // LANES
    lane_tiles = kb_sz // LANES

    w = w_ref[...]
    w_off = IDX_DIM

    def score_block(kb, carry):
        kd = kd_ref[kb]
        acc = jnp.zeros((bq, kb_sz), F32)
        for p in range(n_pairs):
            d2 = jnp.dot(qi_ref[:, p * LANES:(p + 1) * LANES], kd, preferred_element_type=F32)
            we = w[:, w_off + 2 * p:w_off + 2 * p + 1]
            wo = w[:, w_off + 2 * p + 1:w_off + 2 * p + 2]
            acc = acc + jnp.maximum(d2[:, :kb_sz], 0.0) * we + jnp.maximum(d2[:, kb_sz:], 0.0) * wo
        bits = lax.bitcast_convert_type(acc, I32)
        key = jnp.where(bits < 0, bits ^ jnp.int32(0x7FFFFFFF), bits)
        kpos = kb * kb_sz + lax.broadcasted_iota(I32, (bq, kb_sz), 1)
        qpos = qblk * bq + lax.broadcasted_iota(I32, (bq, kb_sz), 0)
        key_scr[kb] = jnp.where(kpos <= qpos, key, jnp.int32(INT_MIN))
        return carry

    lax.fori_loop(0, nkb, score_block, 0)

    def count_ge(cand):
        def body(kb, c):
            key = key_scr[kb]
            for t in range(lane_tiles):
                c = c + (key[:, t * LANES:(t + 1) * LANES] >= cand).astype(I32)
            return c
        c = lax.fori_loop(0, nkb, body, jnp.zeros((bq, LANES), I32))
        return jnp.broadcast_to(jnp.sum(c, axis=1, keepdims=True), (bq, LANES))

    def bit_step(it, ans):
        cand_u = ans | lax.shift_left(jnp.int32(1), 31 - it)
        cnt = count_ge(cand_u ^ jnp.int32(INT_MIN))
        return jnp.where(cnt >= k_sel, cand_u, ans)

    ans = lax.fori_loop(0, 32, bit_step, jnp.zeros((bq, LANES), I32))
    thr = jnp.maximum(ans ^ jnp.int32(INT_MIN), jnp.int32(INT_MIN + 1))
    thr_rows = jnp.concatenate([thr] * lane_tiles, axis=1)

    scale = HEAD_DIM ** -0.5
    for g in range(n_groups):
        qs = jnp.concatenate(
            [q_ref[:, (g * rep + r) * HEAD_DIM:(g * rep + r + 1) * HEAD_DIM] for r in range(rep)], axis=0)
        m_scr[...] = jnp.full_like(m_scr, NEG_BIG)
        l_scr[...] = jnp.zeros_like(l_scr)
        acc_scr[...] = jnp.zeros_like(acc_scr)

        def kv_block(kb, carry):
            rows = pl.ds(pl.multiple_of(kb * kb_sz, kb_sz), kb_sz)
            kblk = k_ref[rows, g * HEAD_DIM:(g + 1) * HEAD_DIM]
            vblk = v_ref[rows, g * HEAD_DIM:(g + 1) * HEAD_DIM]
            s = lax.dot_general(qs, kblk, (((1,), (1,)), ((), ())), preferred_element_type=F32) * scale
            sel = key_scr[kb] >= thr_rows
            s = jnp.where(jnp.concatenate([sel] * rep, axis=0), s, NEG_BIG)
            m_prev = m_scr[...]
            m_new = jnp.maximum(m_prev, jnp.max(s, axis=1, keepdims=True))
            alpha = jnp.exp(m_prev - m_new)
            p = jnp.exp(s - jnp.concatenate([m_new] * lane_tiles, axis=1))
            l_scr[...] = alpha * l_scr[...] + jnp.sum(p, axis=1, keepdims=True)
            acc_scr[...] = alpha * acc_scr[...] + jnp.dot(p.astype(BF16), vblk, preferred_element_type=F32)
            m_scr[...] = m_new
            return carry

        lax.fori_loop(0, nkb, kv_block, 0)
        o = acc_scr[...] / l_scr[...]
        for r in range(rep):
            out_scr[:, (g * rep + r) * HEAD_DIM:(g * rep + r + 1) * HEAD_DIM] = o[r * bq:(r + 1) * bq, :]

    y = out_scr[...]
    o_ref[...] = (y * _rms_scale(y) * g_ref[...]).astype(BF16)


def _dsa(proj, q_off, qi_off, k_off, small, kd, g, d_attn, d_kv, bq):
    bsz, s, _ = proj.shape
    nkb_all, _, kb2 = kd.shape[1:]
    rep = d_attn // d_kv
    k_sel = min(TOPK_MAX, s // 4)
    q_blk = q_off // d_attn
    qi_blk = qi_off // (N_IDX_HEADS * IDX_DIM)
    k_blk = k_off // d_kv
    kern = functools.partial(_dsa_kernel, k_sel=k_sel, rep=rep)
    return pl.pallas_call(
        kern,
        out_shape=jax.ShapeDtypeStruct((bsz, s, d_attn), BF16),
        grid=(bsz, s // bq),
        in_specs=[pl.BlockSpec((None, bq, d_attn), lambda b, i: (b, i, q_blk)),
                  pl.BlockSpec((None, bq, N_IDX_HEADS * IDX_DIM), lambda b, i: (b, i, qi_blk)),
                  pl.BlockSpec((None, s, d_kv), lambda b, i: (b, 0, k_blk)),
                  pl.BlockSpec((None, s, d_kv), lambda b, i: (b, 0, k_blk + 1)),
                  pl.BlockSpec((None, nkb_all, LANES, kb2), lambda b, i: (b, 0, 0, 0)),
                  pl.BlockSpec((None, bq, LANES), lambda b, i: (b, i, 0)),
                  pl.BlockSpec((1, d_attn), lambda b, i: (0, 0))],
        out_specs=pl.BlockSpec((None, bq, d_attn), lambda b, i: (b, i, 0)),
        scratch_shapes=[pltpu.VMEM((nkb_all, bq, kb2 // 2), I32),
                        pltpu.VMEM((rep * bq, LANES), F32),
                        pltpu.VMEM((rep * bq, LANES), F32),
                        pltpu.VMEM((rep * bq, HEAD_DIM), F32),
                        pltpu.VMEM((bq, d_attn), F32)],
        compiler_params=_cparams(("arbitrary", "arbitrary")),
        name="dsa",
    )(proj, proj, proj, proj, kd, small, g)


def _outproj_kernel(a1_ref, a2_ref, w1_ref, w2_ref, x_ref, gpost_ref, gt_ref, gpre_ref, sc_ref, sh_ref,
                    x1_ref, h2_ref, acc_ref):
    j = pl.program_id(2)
    nj = pl.num_programs(2)
    acc_ref[j] = (jnp.dot(a1_ref[...], w1_ref[...], preferred_element_type=F32)
                  + jnp.dot(a2_ref[...], w2_ref[...], preferred_element_type=F32))

    @pl.when(j == nj - 1)
    def _():
        n_tiles, bm, tn = acc_ref.shape
        d = n_tiles * tn
        ssq = jnp.zeros((bm, 1), F32)
        for t in range(n_tiles):
            m = acc_ref[t]
            ssq = ssq + jnp.sum(m * m, axis=-1, keepdims=True)
        rs = lax.rsqrt(ssq / d + RMS_EPS)
        ssq1 = jnp.zeros((bm, 1), F32)
        for t in range(n_tiles):
            cols = slice(t * tn, (t + 1) * tn)
            x1 = x_ref[:, cols] + gt_ref[:, cols] * (acc_ref[t] * rs * gpost_ref[:, cols])
            x1_ref[:, cols] = x1
            ssq1 = ssq1 + jnp.sum(x1 * x1, axis=-1, keepdims=True)
        rs1 = lax.rsqrt(ssq1 / d + RMS_EPS)
        for t in range(n_tiles):
            cols = slice(t * tn, (t + 1) * tn)
            y = x1_ref[:, cols] * rs1 * gpre_ref[:, cols]
            h2_ref[:, cols] = (y * (1.0 + sc_ref[:, cols]) + sh_ref[:, cols]).astype(BF16)


def _out_proj(a1, a2, w1, w2, x, gpost, gt, gpre, sc, sh, bm, tn):
    bsz, s, d = x.shape
    k1, k2 = a1.shape[2], a2.shape[2]
    vec = lambda: pl.BlockSpec((1, d), lambda b, i, j: (0, 0))
    bvec = lambda: pl.BlockSpec((None, 1, d), lambda b, i, j: (b, 0, 0))
    return pl.pallas_call(
        _outproj_kernel,
        out_shape=(jax.ShapeDtypeStruct((bsz, s, d), F32), jax.ShapeDtypeStruct((bsz, s, d), BF16)),
        grid=(bsz, s // bm, d // tn),
        in_specs=[pl.BlockSpec((None, bm, k1), lambda b, i, j: (b, i, 0)),
                  pl.BlockSpec((None, bm, k2), lambda b, i, j: (b, i, 0)),
                  pl.BlockSpec((k1, tn), lambda b, i, j: (0, j)),
                  pl.BlockSpec((k2, tn), lambda b, i, j: (0, j)),
                  pl.BlockSpec((None, bm, d), lambda b, i, j: (b, i, 0), pipeline_mode=pl.Buffered(1)),
                  vec(), bvec(), vec(), bvec(), bvec()],
        out_specs=(pl.BlockSpec((None, bm, d), lambda b, i, j: (b, i, 0), pipeline_mode=pl.Buffered(1)),
                   pl.BlockSpec((None, bm, d), lambda b, i, j: (b, i, 0), pipeline_mode=pl.Buffered(1))),
        scratch_shapes=[pltpu.VMEM((d // tn, bm, tn), F32)],
        compiler_params=_cparams(("arbitrary", "arbitrary", "arbitrary")),
        name="out_proj",
    )(a1, a2, w1, w2, x, gpost, gt, gpre, sc, sh)


def _ffn_kernel(h_ref, wgu_ref, cw_ref, cb_ref, wo_ref, x_ref, g_ref, gt_ref, o_ref, carry_ref):
    i = pl.program_id(1)
    j = pl.program_id(2)
    nj = pl.num_programs(2)
    bm = h_ref.shape[0]
    tf = wo_ref.shape[0]

    u = jnp.dot(h_ref[...], wgu_ref[...], preferred_element_type=F32)
    @pl.when(i == 0)
    def _():
        carry_ref[j] = jnp.zeros(carry_ref.shape[1:], F32)

    prev = carry_ref[j]
    carry_ref[j] = u[bm - SUBLANES:, :]
    cw = cw_ref[...]
    uc = (cb_ref[...] + cw[2:3, :] * u + cw[1:2, :] * _shift_rows(u, prev, 1)
          + cw[0:1, :] * _shift_rows(u, prev, 2))
    act = (_gelu_tanh(uc[:, :tf]) * uc[:, tf:]).astype(BF16)

    @pl.when(j == 0)
    def _():
        o_ref[...] = jnp.zeros_like(o_ref)

    o_ref[...] += jnp.dot(act, wo_ref[...], preferred_element_type=F32)

    @pl.when(j == nj - 1)
    def _():
        f = o_ref[...]
        o_ref[...] = x_ref[...] + gt_ref[...] * (f * _rms_scale(f) * g_ref[...])


def _conv_ffn(h, wgu, cw, cb, wo, x, g, gt, bm, tf):
    bsz, s, d = x.shape
    nj = wo.shape[0] // tf
    return pl.pallas_call(
        _ffn_kernel,
        out_shape=jax.ShapeDtypeStruct((bsz, s, d), F32),
        grid=(bsz, s // bm, nj),
        in_specs=[pl.BlockSpec((None, bm, d), lambda b, i, j: (b, i, 0), pipeline_mode=pl.Buffered(1)),
                  pl.BlockSpec((d, 2 * tf), lambda b, i, j: (0, j)),
                  pl.BlockSpec((cw.shape[0], 2 * tf), lambda b, i, j: (0, j)),
                  pl.BlockSpec((1, 2 * tf), lambda b, i, j: (0, j)),
                  pl.BlockSpec((tf, d), lambda b, i, j: (j, 0)),
                  pl.BlockSpec((None, bm, d), lambda b, i, j: (b, i, 0), pipeline_mode=pl.Buffered(1)),
                  pl.BlockSpec((1, d), lambda b, i, j: (0, 0)),
                  pl.BlockSpec((None, 1, d), lambda b, i, j: (b, 0, 0))],
        out_specs=pl.BlockSpec((None, bm, d), lambda b, i, j: (b, i, 0), pipeline_mode=pl.Buffered(1)),
        scratch_shapes=[pltpu.VMEM((nj, SUBLANES, 2 * tf), F32)],
        compiler_params=_cparams(("arbitrary", "arbitrary", "arbitrary")),
        name="conv_ffn",
    )(h, wgu, cw, cb, wo, x, g, gt)


def _interleave_tiles(a, b, tf):
    r, f = a.shape
    return jnp.stack([a.reshape(r, f // tf, tf), b.reshape(r, f // tf, tf)], axis=2).reshape(r, 2 * f)


def _layer(x, c, w_ada, b_ada, g_pre_mix, g_post_mix, g_pre_ffn, g_post_ffn, w_in,
           conv_lru_w, conv_lru_b, w_rg_a, b_rg_a, w_rg_x, b_rg_x, lru_lambda,
           g_grp_lru, g_grp_attn, w_out, w_ffn_in, conv_ffn_w, conv_ffn_b, w_ffn_out,
           *, bm_in, bn_in, bt_lru, bq, kb_sz, bm_out, tn_out, bm_ffn, tf):
    bsz, s, d = x.shape
    d_lru = w_rg_a.shape[0] * LRU_BLOCK
    d_attn = g_grp_attn.shape[0]
    d_qi = N_IDX_HEADS * IDX_DIM
    d_kv = (w_in.shape[1] - 2 * d_lru - d_attn - d_qi - IDX_DIM - N_IDX_HEADS) // 2
    d_ff = w_ffn_out.shape[0]
    row = lambda v: v.reshape(1, -1)

    mod = _modulation(c, w_ada, b_ada)
    sh1, sc1, gt1, sh2, sc2, gt2 = [m.reshape(bsz, 1, d) for m in jnp.split(mod, 6, axis=-1)]

    o3 = 2 * d_lru + d_attn
    o5 = o3 + 2 * d_kv
    o6 = o5 + d_qi
    w_main = jnp.concatenate([w_in[:, o5:o6], w_in[:, :o5]], axis=1).astype(BF16)
    qi_off, x_off, q_off, k_off = 0, d_qi, d_qi + 2 * d_lru, d_qi + o3
    n_small = w_in.shape[1] - o6
    w_small = jnp.concatenate([w_in[:, o6:], jnp.zeros((d, LANES - n_small), F32)], axis=1).astype(BF16)
    w_out1 = w_out[:d_lru].astype(BF16)
    w_out2 = w_out[d_lru:].astype(BF16)
    wgu = _interleave_tiles(w_ffn_in[:, :d_ff], w_ffn_in[:, d_ff:], tf).astype(BF16)
    cwf = _interleave_tiles(conv_ffn_w[:, :d_ff], conv_ffn_w[:, d_ff:], tf)
    cbf = _interleave_tiles(row(conv_ffn_b)[:, :d_ff], row(conv_ffn_b)[:, d_ff:], tf)
    w_fo = w_ffn_out.astype(BF16)

    proj, small = _in_proj(x, row(g_pre_mix), sc1, sh1, w_main, w_small, bm_in, bn_in)

    y_lru = _rglru(proj, x_off, conv_lru_w, row(conv_lru_b), w_rg_a.astype(BF16), row(b_rg_a),
                   w_rg_x.astype(BF16), row(b_rg_x), row(lru_lambda), row(g_grp_lru), bt_lru)

    kt = jnp.swapaxes(small[:, :, :IDX_DIM].astype(BF16).reshape(bsz, s // kb_sz, kb_sz, IDX_DIM), 2, 3)
    zz = jnp.zeros_like(kt)
    kd = jnp.concatenate([jnp.concatenate([kt, zz], axis=3), jnp.concatenate([zz, kt], axis=3)], axis=2)
    y_attn = _dsa(proj, q_off, qi_off, k_off, small, kd, row(g_grp_attn), d_attn, d_kv, bq)

    x1, h2 = _out_proj(y_lru, y_attn, w_out1, w_out2, x, row(g_post_mix), gt1, row(g_pre_ffn), sc2, sh2,
                       bm_out, tn_out)

    return _conv_ffn(h2, wgu, cwf, cbf, w_fo, x1, row(g_post_ffn), gt2, bm_ffn, tf)


_TILES = dict(bm_in=512, bn_in=1024, bt_lru=256, bq=256, kb_sz=512, bm_out=512, tn_out=512, bm_ffn=512, tf=256)


def kernel(x, c, w_ada, b_ada, g_pre_mix, g_post_mix, g_pre_ffn, g_post_ffn, w_in, conv_lru_w, conv_lru_b,
           w_rg_a, b_rg_a, w_rg_x, b_rg_x, lru_lambda, g_grp_lru, g_grp_attn, w_out, w_ffn_in,
           conv_ffn_w, conv_ffn_b, w_ffn_out):
    for l in range(w_ada.shape[0]):
        x = _layer(x, c, w_ada[l], b_ada[l], g_pre_mix[l], g_post_mix[l], g_pre_ffn[l], g_post_ffn[l],
                   w_in[l], conv_lru_w[l], conv_lru_b[l], w_rg_a[l], b_rg_a[l], w_rg_x[l], b_rg_x[l],
                   lru_lambda[l], g_grp_lru[l], g_grp_attn[l], w_out[l], w_ffn_in[l], conv_ffn_w[l],
                   conv_ffn_b[l], w_ffn_out[l], **_TILES)
    return x
```

```python
import functools
import math

import jax
import jax.numpy as jnp
from jax import lax
from jax.experimental import pallas as pl
from jax.experimental.pallas import tpu as pltpu

F32 = jnp.float32
BF16 = jnp.bfloat16
I32 = jnp.int32

RMS_EPS = 1e-6
LRU_C = 8.0
LRU_BLOCK = 128
HEAD_DIM = 128
N_KV_HEADS = 4
N_IDX_HEADS = 32
IDX_DIM = 64
TOPK_MAX = 256

LANES = 128
SUBLANES = 8
VMEM_LIMIT = 56 * 1024 * 1024

SELECT_ROWS = 128
INT_MIN = -(2 ** 31)
NEG_BIG = -1e30
LOG2_E = 1.4426950408889634


def _cparams(sem, flags=None):
    return pltpu.CompilerParams(dimension_semantics=sem, vmem_limit_bytes=VMEM_LIMIT, flags=flags)


def _rms_scale(v):
    return lax.rsqrt(jnp.mean(v * v, axis=-1, keepdims=True) + RMS_EPS)


def _gelu_tanh(v):
    inner = v * (1.0 + 0.044715 * (v * v))
    return v / (1.0 + jnp.exp((-2.0 * 0.7978845608028654) * inner))


def _shift_rows(cur, prev8, d):
    rolled = pltpu.roll(cur, d, 0)
    head = pltpu.roll(prev8, d, 0)
    row = lax.broadcasted_iota(I32, (SUBLANES, cur.shape[1]), 0)
    top = jnp.where(row < d, head, rolled[0:SUBLANES])
    return jnp.concatenate([top, rolled[SUBLANES:]], axis=0)


def _mod_kernel(c_ref, w_ref, b_ref, o_ref):
    c = c_ref[...]
    s = c * jax.nn.sigmoid(c)
    o_ref[...] = jnp.dot(s, w_ref[...], preferred_element_type=F32,
                         precision=lax.Precision.HIGHEST) + b_ref[...]


def _modulation(c, w_ada, b_ada, tn=512):
    bsz, d = c.shape
    n = w_ada.shape[1]
    rows = -(-bsz // SUBLANES) * SUBLANES
    c_pad = jnp.zeros((rows, d), F32).at[:bsz].set(c)
    out = pl.pallas_call(
        _mod_kernel,
        out_shape=jax.ShapeDtypeStruct((rows, n), F32),
        grid=(n // tn,),
        in_specs=[pl.BlockSpec((rows, d), lambda j: (0, 0)),
                  pl.BlockSpec((d, tn), lambda j: (0, j)),
                  pl.BlockSpec((1, tn), lambda j: (0, j))],
        out_specs=pl.BlockSpec((rows, tn), lambda j: (0, j)),
        compiler_params=_cparams(("arbitrary",)),
        name="modulation",
    )(c_pad, w_ada, b_ada.reshape(1, n))
    return out[:bsz]


def _inproj_kernel(x_ref, g_ref, sc_ref, sh_ref, w_ref, ws_ref, o_ref, os_ref, h_ref):
    @pl.when(pl.program_id(2) == 0)
    def _():
        x = x_ref[...]
        y = x * _rms_scale(x) * g_ref[...]
        h = (y * (1.0 + sc_ref[...]) + sh_ref[...]).astype(BF16)
        h_ref[...] = h
        os_ref[...] = jnp.dot(h, ws_ref[...], preferred_element_type=F32)

    o_ref[...] = jnp.dot(h_ref[...], w_ref[...], preferred_element_type=F32).astype(BF16)


def _in_proj(x, g, sc, sh, w_main, n, w_small, bm, bn):
    bsz, s, d = x.shape
    ns = w_small.shape[1]
    return pl.pallas_call(
        _inproj_kernel,
        out_shape=(jax.ShapeDtypeStruct((bsz, s, n), BF16), jax.ShapeDtypeStruct((bsz, s, ns), F32)),
        grid=(bsz, s // bm, n // bn),
        in_specs=[pl.BlockSpec((None, bm, d), lambda b, i, j: (b, i, 0)),
                  pl.BlockSpec((1, d), lambda b, i, j: (0, 0)),
                  pl.BlockSpec((None, 1, d), lambda b, i, j: (b, 0, 0)),
                  pl.BlockSpec((None, 1, d), lambda b, i, j: (b, 0, 0)),
                  pl.BlockSpec((d, bn), lambda b, i, j: (0, j)),
                  pl.BlockSpec((d, ns), lambda b, i, j: (0, 0))],
        out_specs=(pl.BlockSpec((None, bm, bn), lambda b, i, j: (b, i, j)),
                   pl.BlockSpec((None, bm, ns), lambda b, i, j: (b, i, 0))),
        scratch_shapes=[pltpu.VMEM((bm, d), BF16)],
        compiler_params=_cparams(("arbitrary", "arbitrary", "arbitrary")),
        name="in_proj",
    )(x, g, sc, sh, w_main, w_small)


def _rglru_kernel(xl_ref, gate_ref, cw_ref, cb_ref, wa_ref, ba_ref, wx_ref, bx_ref, lam_ref, g_ref,
                  o_ref, halo_ref, hc_ref, a_scr, b_scr):
    bt, dl = xl_ref.shape
    nblk = dl // LRU_BLOCK

    @pl.when(pl.program_id(1) == 0)
    def _():
        halo_ref[...] = jnp.zeros_like(halo_ref)
        hc_ref[...] = jnp.zeros_like(hc_ref)

    xl = xl_ref[...].astype(F32)
    prev = halo_ref[...]
    halo_ref[...] = xl[bt - SUBLANES:, :]

    cw = cw_ref[...]
    kw = cw.shape[0]
    xc = cb_ref[...] + cw[kw - 1:kw, :] * xl
    for dlt in range(1, kw):
        xc = xc + cw[kw - 1 - dlt:kw - dlt, :] * _shift_rows(xl, prev, dlt)

    xcb = xc.astype(BF16)
    r_parts, i_parts = [], []
    for n in range(nblk):
        blk = xcb[:, n * LRU_BLOCK:(n + 1) * LRU_BLOCK]
        r_parts.append(jnp.dot(blk, wa_ref[n], preferred_element_type=F32))
        i_parts.append(jnp.dot(blk, wx_ref[n], preferred_element_type=F32))
    r = jax.nn.sigmoid(jnp.concatenate(r_parts, axis=1) + ba_ref[...])
    ig = jax.nn.sigmoid(jnp.concatenate(i_parts, axis=1) + bx_ref[...])

    z = -lam_ref[...]
    softplus = jnp.maximum(z, 0.0) + jnp.log(1.0 + jnp.exp(-jnp.abs(z)))
    log_a = (-LRU_C * softplus) * r
    a = jnp.exp(log_a)
    u = jnp.sqrt(1.0 - jnp.exp(2.0 * log_a)) * (ig * xc)

    rowmod = lax.broadcasted_iota(I32, (bt, dl), 0) % SUBLANES
    b = u
    for sft in (1, 2, 4):
        keep = rowmod >= sft
        a_prev = jnp.where(keep, pltpu.roll(a, sft, 0), 1.0)
        b_prev = jnp.where(keep, pltpu.roll(b, sft, 0), 0.0)
        b = a * b_prev + b
        a = a * a_prev
    a_scr[...] = a
    b_scr[...] = b

    def group(gi, carry):
        rows = pl.ds(pl.multiple_of(gi * SUBLANES, SUBLANES), SUBLANES)
        h = a_scr[rows, :] * carry + b_scr[rows, :]
        b_scr[rows, :] = h
        return jnp.broadcast_to(h[SUBLANES - 1:SUBLANES, :], (SUBLANES, dl))

    hc_ref[...] = lax.fori_loop(0, bt // SUBLANES, group, hc_ref[...])

    y = b_scr[...] * _gelu_tanh(gate_ref[...].astype(F32))
    o_ref[...] = (y * _rms_scale(y) * g_ref[...]).astype(BF16)


def _rglru(proj, x_off, cw, cb, wa, ba, wx, bx, lam, g, bt):
    bsz, s, _ = proj.shape
    dl = cw.shape[1]
    nblk = dl // LRU_BLOCK
    x_blk = x_off // dl
    vec = lambda: pl.BlockSpec((1, dl), lambda b, t: (0, 0))
    return pl.pallas_call(
        _rglru_kernel,
        out_shape=jax.ShapeDtypeStruct((bsz, s, dl), BF16),
        grid=(bsz, s // bt),
        in_specs=[pl.BlockSpec((None, bt, dl), lambda b, t: (b, t, x_blk)),
                  pl.BlockSpec((None, bt, dl), lambda b, t: (b, t, x_blk + 1)),
                  pl.BlockSpec((cw.shape[0], dl), lambda b, t: (0, 0)),
                  vec(),
                  pl.BlockSpec((nblk, LRU_BLOCK, LRU_BLOCK), lambda b, t: (0, 0, 0)),
                  vec(),
                  pl.BlockSpec((nblk, LRU_BLOCK, LRU_BLOCK), lambda b, t: (0, 0, 0)),
                  vec(), vec(), vec()],
        out_specs=pl.BlockSpec((None, bt, dl), lambda b, t: (b, t, 0)),
        scratch_shapes=[pltpu.VMEM((SUBLANES, dl), F32), pltpu.VMEM((SUBLANES, dl), F32),
                        pltpu.VMEM((bt, dl), F32), pltpu.VMEM((bt, dl), F32)],
        compiler_params=_cparams(("arbitrary", "arbitrary")),
        name="rglru",
    )(proj, proj, cw, cb, wa, ba, wx, bx, lam, g)


def _dsa_kernel(q_ref, *refs, k_sel, rep, n_qi):
    qi_refs = refs[:n_qi]
    (k_ref, v_ref, kd_ref, w_ref, g_ref, o_ref,
     key_scr, m_scr, l_scr, acc_scr, out_scr) = refs[n_qi:]
    bq = q_ref.shape[0]
    nkb_all, _, kb2 = kd_ref.shape
    kb_sz = kb2 // 2
    n_groups = q_ref.shape[1] // (rep * HEAD_DIM)
    qblk = pl.program_id(1)
    nkb = lax.div((qblk + 1) * bq + kb_sz - 1, kb_sz)
    qi_w = qi_refs[0].shape[1]
    n_pairs = n_qi * qi_w // LANES
    lane_tiles = kb_sz // LANES
    tile = lambda v: jnp.concatenate([v] * lane_tiles, axis=1)

    w = w_ref[...]
    w_off = IDX_DIM

    def score_block(kb, carry):
        kd = kd_ref[kb]
        acc = jnp.zeros((bq, kb_sz), F32)
        for p in range(n_pairs):
            piece, col = divmod(p * LANES, qi_w)
            d2 = jnp.dot(qi_refs[piece][:, col:col + LANES], kd, preferred_element_type=F32)
            we = w[:, w_off + 2 * p:w_off + 2 * p + 1]
            wo = w[:, w_off + 2 * p + 1:w_off + 2 * p + 2]
            acc = acc + jnp.maximum(d2[:, :kb_sz], 0.0) * we + jnp.maximum(d2[:, kb_sz:], 0.0) * wo
        bits = lax.bitcast_convert_type(acc, I32)
        key = jnp.where(bits < 0, bits ^ jnp.int32(0x7FFFFFFF), bits)
        kpos = kb * kb_sz + lax.broadcasted_iota(I32, (bq, kb_sz), 1)
        qpos = qblk * bq + lax.broadcasted_iota(I32, (bq, kb_sz), 0)
        key_scr[kb] = jnp.where(kpos <= qpos, key, jnp.int32(INT_MIN))
        return carry

    lax.fori_loop(0, nkb, score_block, 0)

    def count_lanes(hit_fn, rows):
        def body(kb, c):
            hit = hit_fn(kb)
            for t in range(lane_tiles):
                c = c + hit[:, t * LANES:(t + 1) * LANES].astype(I32)
            return c
        return lax.fori_loop(0, nkb, body, jnp.zeros((rows, LANES), I32))

    def lane_total(c):
        return jnp.broadcast_to(jnp.sum(c, axis=1, keepdims=True), c.shape)

    def count_rows(hit_fn, rows):
        return lane_total(count_lanes(hit_fn, rows))

    def bit_step(it, carry):
        bit = lax.shift_left(jnp.int32(1), 31 - it)
        cands, cnts = [], []
        for part, (ans, _) in enumerate(carry):
            rows = slice(part * SELECT_ROWS, (part + 1) * SELECT_ROWS)
            cand_u = ans | bit
            cand_rows = tile(cand_u ^ jnp.int32(INT_MIN))
            cands.append(cand_u)
            cnts.append(count_lanes(lambda kb, rows=rows, cand_rows=cand_rows: key_scr[kb, rows, :] >= cand_rows,
                                    SELECT_ROWS))
        out = []
        for (ans, n_ge), cand_u, lanes in zip(carry, cands, cnts):
            cnt = lane_total(lanes)
            ok = cnt >= k_sel
            out.append((jnp.where(ok, cand_u, ans), jnp.where(ok, cnt, n_ge)))
        return tuple(out)

    zero = jnp.zeros((SELECT_ROWS, LANES), I32)
    parts = lax.fori_loop(0, 32, bit_step, tuple((zero, zero) for _ in range(bq // SELECT_ROWS)))
    ans = jnp.concatenate([p[0] for p in parts], axis=0)
    n_ge = jnp.concatenate([p[1] for p in parts], axis=0)
    thr = jnp.maximum(ans ^ jnp.int32(INT_MIN), jnp.int32(INT_MIN + 1))
    thr_rows = tile(thr)

    def count_ge(cand):
        cand_rows = tile(cand)
        return count_rows(lambda kb: key_scr[kb] >= cand_rows, bq)

    @pl.when(jnp.max(n_ge) > k_sel)
    def _():
        need = k_sel - count_ge(thr + 1)
        pos_bits = (nkb_all * kb_sz - 1).bit_length()

        def ties_before(pos):
            pos_rows = tile(pos)

            def hit(kb):
                kpos = kb * kb_sz + lax.broadcasted_iota(I32, (bq, kb_sz), 1)
                return (key_scr[kb] == thr_rows) & (kpos < pos_rows)
            return count_rows(hit, bq)

        def pos_step(it, last):
            cand = last | lax.shift_left(jnp.int32(1), pos_bits - 1 - it)
            return jnp.where(ties_before(cand) < need, cand, last)

        last_rows = tile(lax.fori_loop(0, pos_bits, pos_step, jnp.zeros((bq, LANES), I32)))

        def demote(kb, carry):
            key = key_scr[kb]
            kpos = kb * kb_sz + lax.broadcasted_iota(I32, (bq, kb_sz), 1)
            key_scr[kb] = jnp.where((key == thr_rows) & (kpos > last_rows), key - 1, key)
            return carry

        lax.fori_loop(0, nkb, demote, 0)

    c_exp = (HEAD_DIM ** -0.5) * LOG2_E
    for g in range(n_groups):
        qs = jnp.concatenate(
            [q_ref[:, (g * rep + r) * HEAD_DIM:(g * rep + r + 1) * HEAD_DIM] for r in range(rep)], axis=0)
        m_scr[...] = jnp.full_like(m_scr, NEG_BIG)
        l_scr[...] = jnp.zeros_like(l_scr)
        acc_scr[...] = jnp.zeros_like(acc_scr)

        def kv_block(kb, carry):
            rows = pl.ds(pl.multiple_of(kb * kb_sz, kb_sz), kb_sz)
            kblk = k_ref[rows, g * HEAD_DIM:(g + 1) * HEAD_DIM]
            vblk = v_ref[rows, g * HEAD_DIM:(g + 1) * HEAD_DIM]
            s = lax.dot_general(qs, kblk, (((1,), (1,)), ((), ())), preferred_element_type=F32)
            sel = key_scr[kb] >= thr_rows
            s = jnp.where(jnp.concatenate([sel] * rep, axis=0), s, NEG_BIG)
            m_prev = m_scr[...]
            m_new = jnp.maximum(m_prev, jnp.max(s, axis=1, keepdims=True))
            alpha = jnp.exp2((m_prev - m_new) * c_exp)
            p = jnp.exp2((s - tile(m_new)) * c_exp)
            l_scr[...] = alpha * l_scr[...] + jnp.sum(p, axis=1, keepdims=True)
            acc_scr[...] = alpha * acc_scr[...] + jnp.dot(p.astype(BF16), vblk, preferred_element_type=F32)
            m_scr[...] = m_new
            return carry

        lax.fori_loop(0, nkb, kv_block, 0)
        o = acc_scr[...] / l_scr[...]
        for r in range(rep):
            out_scr[:, (g * rep + r) * HEAD_DIM:(g * rep + r + 1) * HEAD_DIM] = o[r * bq:(r + 1) * bq, :]

    y = out_scr[...]
    o_ref[...] = (y * _rms_scale(y) * g_ref[...]).astype(BF16)


def _dsa(proj, q_off, qi_off, k_off, small, kd, g, d_attn, d_kv, bq):
    bsz, s, _ = proj.shape
    nkb_all, _, kb2 = kd.shape[1:]
    rep = d_attn // d_kv
    k_sel = min(TOPK_MAX, s // 4)
    d_qi = N_IDX_HEADS * IDX_DIM
    qi_w = math.gcd(qi_off, d_qi)
    n_qi = d_qi // qi_w
    q_blk = q_off // d_attn
    k_blk = k_off // d_kv
    kern = functools.partial(_dsa_kernel, k_sel=k_sel, rep=rep, n_qi=n_qi)
    qi_specs = [pl.BlockSpec((None, bq, qi_w), lambda b, i, t=t: (b, i, qi_off // qi_w + t)) for t in range(n_qi)]
    return pl.pallas_call(
        kern,
        out_shape=jax.ShapeDtypeStruct((bsz, s, d_attn), BF16),
        grid=(bsz, s // bq),
        in_specs=[pl.BlockSpec((None, bq, d_attn), lambda b, i: (b, i, q_blk))] + qi_specs + [
                  pl.BlockSpec((None, s, d_kv), lambda b, i: (b, 0, k_blk)),
                  pl.BlockSpec((None, s, d_kv), lambda b, i: (b, 0, k_blk + 1)),
                  pl.BlockSpec((None, nkb_all, LANES, kb2), lambda b, i: (b, 0, 0, 0)),
                  pl.BlockSpec((None, bq, LANES), lambda b, i: (b, i, 0)),
                  pl.BlockSpec((1, d_attn), lambda b, i: (0, 0))],
        out_specs=pl.BlockSpec((None, bq, d_attn), lambda b, i: (b, i, 0)),
        scratch_shapes=[pltpu.VMEM((nkb_all, bq, kb2 // 2), I32),
                        pltpu.VMEM((rep * bq, LANES), F32),
                        pltpu.VMEM((rep * bq, LANES), F32),
                        pltpu.VMEM((rep * bq, HEAD_DIM), F32),
                        pltpu.VMEM((bq, d_attn), F32)],
        compiler_params=_cparams(("arbitrary", "arbitrary")),
        name="dsa",
    )(proj, *([proj] * n_qi), proj, proj, kd, small, g)


def _outproj_kernel(a1_ref, a2_ref, w1_ref, w2_ref, x_ref, gpost_ref, gt_ref, gpre_ref, sc_ref, sh_ref,
                    x1_ref, h2_ref, acc_ref):
    j = pl.program_id(2)
    nj = pl.num_programs(2)
    acc_ref[j] = (jnp.dot(a1_ref[...], w1_ref[...], preferred_element_type=F32)
                  + jnp.dot(a2_ref[...], w2_ref[...], preferred_element_type=F32))

    @pl.when(j == nj - 1)
    def _():
        n_tiles, bm, tn = acc_ref.shape
        d = n_tiles * tn
        ssq = jnp.zeros((bm, 1), F32)
        for t in range(n_tiles):
            m = acc_ref[t]
            ssq = ssq + jnp.sum(m * m, axis=-1, keepdims=True)
        rs = lax.rsqrt(ssq / d + RMS_EPS)
        ssq1 = jnp.zeros((bm, 1), F32)
        for t in range(n_tiles):
            cols = slice(t * tn, (t + 1) * tn)
            x1 = x_ref[:, cols] + gt_ref[:, cols] * (acc_ref[t] * rs * gpost_ref[:, cols])
            x1_ref[:, cols] = x1
            ssq1 = ssq1 + jnp.sum(x1 * x1, axis=-1, keepdims=True)
        rs1 = lax.rsqrt(ssq1 / d + RMS_EPS)
        for t in range(n_tiles):
            cols = slice(t * tn, (t + 1) * tn)
            y = x1_ref[:, cols] * rs1 * gpre_ref[:, cols]
            h2_ref[:, cols] = (y * (1.0 + sc_ref[:, cols]) + sh_ref[:, cols]).astype(BF16)


def _out_proj(a1, a2, w, x, gpost, gt, gpre, sc, sh, bm, tn):
    bsz, s, d = x.shape
    k1, k2 = a1.shape[2], a2.shape[2]
    k2_blk = k1 // k2
    vec = lambda: pl.BlockSpec((1, d), lambda b, i, j: (0, 0))
    bvec = lambda: pl.BlockSpec((None, 1, d), lambda b, i, j: (b, 0, 0))
    big = lambda: pl.BlockSpec((None, bm, d), lambda b, i, j: (b, i, 0), pipeline_mode=pl.Buffered(1))
    return pl.pallas_call(
        _outproj_kernel,
        out_shape=(jax.ShapeDtypeStruct((bsz, s, d), F32), jax.ShapeDtypeStruct((bsz, s, d), BF16)),
        grid=(bsz, s // bm, d // tn),
        in_specs=[pl.BlockSpec((None, bm, k1), lambda b, i, j: (b, i, 0)),
                  pl.BlockSpec((None, bm, k2), lambda b, i, j: (b, i, 0)),
                  pl.BlockSpec((k1, tn), lambda b, i, j: (0, j)),
                  pl.BlockSpec((k2, tn), lambda b, i, j: (k2_blk, j)),
                  big(), vec(), bvec(), vec(), bvec(), bvec()],
        out_specs=(big(), big()),
        scratch_shapes=[pltpu.VMEM((d // tn, bm, tn), F32)],
        compiler_params=_cparams(("arbitrary", "arbitrary", "arbitrary")),
        name="out_proj",
    )(a1, a2, w, w, x, gpost, gt, gpre, sc, sh)


FFN_ROW_CHUNK = 64
FFN_K_SPLIT = 4


def _ffn_kernel(h_ref, wg_ref, wu_ref, cwg_ref, cwu_ref, cbg_ref, cbu_ref, wo_ref, x_ref, g_ref, gt_ref,
                o_ref, acc_ref, u_even, u_odd, act_even, act_odd, carry_ref):
    i = pl.program_id(1)
    s = pl.program_id(2)
    nj = pl.num_programs(2) - 2
    bm, d = h_ref.shape
    tf = wo_ref.shape[0]
    tile = jnp.minimum(s, nj - 1)

    @pl.when(s == 0)
    def _():
        acc_ref[...] = jnp.zeros_like(acc_ref)
        act_even[...] = jnp.zeros_like(act_even)
        u_odd[...] = jnp.zeros_like(u_odd)

    @pl.when(i == 0)
    def _():
        carry_ref[tile] = jnp.zeros(carry_ref.shape[1:], F32)

    def step(u_new, u_last, act_new, act_last):
        live = (s > 0).astype(F32)
        cwg, cwu, cbg, cbu = cwg_ref[...], cwu_ref[...], cbg_ref[...], cbu_ref[...]

        def conv(r0, c0, cw, cb):
            cols = slice(c0, c0 + tf)
            return (cb + cw[2:3, :] * u_last[r0:r0 + FFN_ROW_CHUNK, cols]
                    + cw[1:2, :] * u_last[r0 - 1:r0 - 1 + FFN_ROW_CHUNK, cols]
                    + cw[0:1, :] * u_last[r0 - 2:r0 - 2 + FFN_ROW_CHUNK, cols])

        def finish_rows(c, after):
            bits = lax.bitcast_convert_type(after[0:1, 0:tf], jnp.uint32)
            zero = lax.shift_right_logical(lax.shift_right_logical(bits, jnp.uint32(16)), jnp.uint32(16)).astype(F32)
            r0 = SUBLANES + c * FFN_ROW_CHUNK
            gate = conv(r0, 0, cwg, cbg + zero)
            up = conv(r0, tf, cwu, cbu + zero)
            act_new[c * FFN_ROW_CHUNK:(c + 1) * FFN_ROW_CHUNK, :] = (_gelu_tanh(gate) * up * live).astype(BF16)

        n_chunks = bm // FFN_ROW_CHUNK
        nw = d // n_chunks
        act_l = act_last[...]
        for c in range(n_chunks):
            cols = slice(c * nw, (c + 1) * nw)
            piece = jnp.dot(act_l, wo_ref[:, cols], preferred_element_type=F32)
            acc_ref[:, cols] += piece
            finish_rows(c, piece)

        w_gu = jnp.concatenate([wg_ref[...], wu_ref[...]], axis=1)
        u = jnp.dot(h_ref[...], w_gu, preferred_element_type=F32)
        u_new[0:SUBLANES, :] = carry_ref[tile]
        u_new[SUBLANES:, :] = u
        carry_ref[tile] = u[bm - SUBLANES:, :]

    parity = lax.rem(s, 2)

    @pl.when(parity == 0)
    def _():
        step(u_even, u_odd, act_odd, act_even)

    @pl.when(parity == 1)
    def _():
        step(u_odd, u_even, act_even, act_odd)

    @pl.when(s == nj + 1)
    def _():
        scale = gt_ref[...] * g_ref[...]

        def rows_out(r, carry):
            rows = pl.ds(pl.multiple_of(r * SUBLANES, SUBLANES), SUBLANES)
            f = acc_ref[rows, :]
            o_ref[rows, :] = x_ref[rows, :] + scale * (f * _rms_scale(f))
            return carry

        lax.fori_loop(0, bm // SUBLANES, rows_out, 0, unroll=8)


def _conv_ffn(h, w_in, cw, cb, wo, x, g, gt, bm, tf):
    bsz, s, d = x.shape
    nj = wo.shape[0] // tf
    kw = cw.shape[0]
    up_t = lambda j: jnp.minimum(j, nj - 1)
    ew_t = lambda j: jnp.clip(j - 1, 0, nj - 1)
    dn_t = lambda j: jnp.clip(j - 2, 0, nj - 1)
    big = lambda: pl.BlockSpec((None, bm, d), lambda b, i, j: (b, i, 0), pipeline_mode=pl.Buffered(1))
    return pl.pallas_call(
        _ffn_kernel,
        out_shape=jax.ShapeDtypeStruct((bsz, s, d), F32),
        grid=(bsz, s // bm, nj + 2),
        in_specs=[big(),
                  pl.BlockSpec((d, tf), lambda b, i, j: (0, up_t(j))),
                  pl.BlockSpec((d, tf), lambda b, i, j: (0, nj + up_t(j))),
                  pl.BlockSpec((kw, tf), lambda b, i, j: (0, ew_t(j))),
                  pl.BlockSpec((kw, tf), lambda b, i, j: (0, nj + ew_t(j))),
                  pl.BlockSpec((1, tf), lambda b, i, j: (0, ew_t(j))),
                  pl.BlockSpec((1, tf), lambda b, i, j: (0, nj + ew_t(j))),
                  pl.BlockSpec((tf, d), lambda b, i, j: (dn_t(j), 0)),
                  big(),
                  pl.BlockSpec((1, d), lambda b, i, j: (0, 0)),
                  pl.BlockSpec((None, 1, d), lambda b, i, j: (b, 0, 0))],
        out_specs=big(),
        scratch_shapes=[pltpu.VMEM((bm, d), F32),
                        pltpu.VMEM((SUBLANES + bm, 2 * tf), F32),
                        pltpu.VMEM((SUBLANES + bm, 2 * tf), F32),
                        pltpu.VMEM((bm, tf), BF16),
                        pltpu.VMEM((bm, tf), BF16),
                        pltpu.VMEM((nj, SUBLANES, 2 * tf), F32)],
        compiler_params=_cparams(("arbitrary", "arbitrary", "arbitrary")),
        name="conv_ffn",
    )(h, w_in, w_in, cw, cw, cb, cb, wo, x, g, gt)


def _layer(x, c, w_ada, b_ada, g_pre_mix, g_post_mix, g_pre_ffn, g_post_ffn, w_in,
           conv_lru_w, conv_lru_b, w_rg_a, b_rg_a, w_rg_x, b_rg_x, lru_lambda,
           g_grp_lru, g_grp_attn, w_out, w_ffn_in, conv_ffn_w, conv_ffn_b, w_ffn_out,
           *, bm_in, bn_in, bt_lru, bq, kb_sz, bm_out, tn_out, bm_ffn, tf):
    bsz, s, d = x.shape
    d_lru = w_rg_a.shape[0] * LRU_BLOCK
    d_attn = g_grp_attn.shape[0]
    d_qi = N_IDX_HEADS * IDX_DIM
    d_kv = (w_in.shape[1] - 2 * d_lru - d_attn - d_qi - IDX_DIM - N_IDX_HEADS) // 2
    row = lambda v: v.reshape(1, -1)

    mod = _modulation(c, w_ada, b_ada)
    sh1, sc1, gt1, sh2, sc2, gt2 = [m.reshape(bsz, 1, d) for m in jnp.split(mod, 6, axis=-1)]

    q_off = 2 * d_lru
    k_off = q_off + d_attn
    qi_off = k_off + 2 * d_kv
    n_main = qi_off + d_qi
    n_small = w_in.shape[1] - n_main
    w_in_b = w_in.astype(BF16)
    w_small = jnp.concatenate([w_in[:, n_main:], jnp.zeros((d, LANES - n_small), F32)], axis=1).astype(BF16)

    proj, small = _in_proj(x, row(g_pre_mix), sc1, sh1, w_in_b, n_main, w_small, bm_in, bn_in)

    y_lru = _rglru(proj, 0, conv_lru_w, row(conv_lru_b), w_rg_a.astype(BF16), row(b_rg_a),
                   w_rg_x.astype(BF16), row(b_rg_x), row(lru_lambda), row(g_grp_lru), bt_lru)

    kt = jnp.swapaxes(small[:, :, :IDX_DIM].astype(BF16).reshape(bsz, s // kb_sz, kb_sz, IDX_DIM), 2, 3)
    zz = jnp.zeros_like(kt)
    kd = jnp.concatenate([jnp.concatenate([kt, zz], axis=3), jnp.concatenate([zz, kt], axis=3)], axis=2)
    y_attn = _dsa(proj, q_off, qi_off, k_off, small, kd, row(g_grp_attn), d_attn, d_kv, bq)

    x1, h2 = _out_proj(y_lru, y_attn, w_out.astype(BF16), x, row(g_post_mix), gt1, row(g_pre_ffn), sc2, sh2,
                       bm_out, tn_out)

    return _conv_ffn(h2, w_ffn_in.astype(BF16), conv_ffn_w, row(conv_ffn_b), w_ffn_out.astype(BF16),
                     x1, row(g_post_ffn), gt2, bm_ffn, tf)


_TILES = dict(bm_in=512, bn_in=1024, bt_lru=256, bq=256, kb_sz=512, bm_out=512, tn_out=512, bm_ffn=512, tf=256)


def kernel(x, c, w_ada, b_ada, g_pre_mix, g_post_mix, g_pre_ffn, g_post_ffn, w_in, conv_lru_w, conv_lru_b,
           w_rg_a, b_rg_a, w_rg_x, b_rg_x, lru_lambda, g_grp_lru, g_grp_attn, w_out, w_ffn_in,
           conv_ffn_w, conv_ffn_b, w_ffn_out):
    for l in range(w_ada.shape[0]):
        x = _layer(x, c, w_ada[l], b_ada[l], g_pre_mix[l], g_post_mix[l], g_pre_ffn[l], g_post_ffn[l],
                   w_in[l], conv_lru_w[l], conv_lru_b[l], w_rg_a[l], b_rg_a[l], w_rg_x[l], b_rg_x[l],
                   lru_lambda[l], g_grp_lru[l], g_grp_attn[l], w_out[l], w_ffn_in[l], conv_ffn_w[l],
                   conv_ffn_b[l], w_ffn_out[l], **_TILES)
    return x
```

```python
import functools
import math

import jax
import jax.numpy as jnp
from jax import lax
from jax.experimental import pallas as pl
from jax.experimental.pallas import tpu as pltpu

F32 = jnp.float32
BF16 = jnp.bfloat16
I32 = jnp.int32

RMS_EPS = 1e-6
LRU_C = 8.0
LRU_BLOCK = 128
HEAD_DIM = 128
N_KV_HEADS = 4
N_IDX_HEADS = 32
IDX_DIM = 64
TOPK_MAX = 256

LANES = 128
SUBLANES = 8
VMEM_LIMIT = 56 * 1024 * 1024

SELECT_ROWS = 128
INT_MIN = -(2 ** 31)
NEG_BIG = -1e30
LOG2_E = 1.4426950408889634


def _cparams(sem, flags=None):
    return pltpu.CompilerParams(dimension_semantics=sem, vmem_limit_bytes=VMEM_LIMIT, flags=flags)


def _rms_scale(v):
    return lax.rsqrt(jnp.mean(v * v, axis=-1, keepdims=True) + RMS_EPS)


def _gelu_tanh(v):
    inner = v * (1.0 + 0.044715 * (v * v))
    return v / (1.0 + jnp.exp((-2.0 * 0.7978845608028654) * inner))


def _shift_rows(cur, prev8, d):
    rolled = pltpu.roll(cur, d, 0)
    head = pltpu.roll(prev8, d, 0)
    row = lax.broadcasted_iota(I32, (SUBLANES, cur.shape[1]), 0)
    top = jnp.where(row < d, head, rolled[0:SUBLANES])
    return jnp.concatenate([top, rolled[SUBLANES:]], axis=0)


def _mod_kernel(c_ref, w_ref, b_ref, o_ref):
    c = c_ref[...]
    s = c * jax.nn.sigmoid(c)
    o_ref[...] = jnp.dot(s, w_ref[...], preferred_element_type=F32,
                         precision=lax.Precision.HIGHEST) + b_ref[...]


def _modulation(c, w_ada, b_ada, tn=512):
    bsz, d = c.shape
    n = w_ada.shape[1]
    rows = -(-bsz // SUBLANES) * SUBLANES
    c_pad = jnp.zeros((rows, d), F32).at[:bsz].set(c)
    out = pl.pallas_call(
        _mod_kernel,
        out_shape=jax.ShapeDtypeStruct((rows, n), F32),
        grid=(n // tn,),
        in_specs=[pl.BlockSpec((rows, d), lambda j: (0, 0)),
                  pl.BlockSpec((d, tn), lambda j: (0, j)),
                  pl.BlockSpec((1, tn), lambda j: (0, j))],
        out_specs=pl.BlockSpec((rows, tn), lambda j: (0, j)),
        compiler_params=_cparams(("arbitrary",)),
        name="modulation",
    )(c_pad, w_ada, b_ada.reshape(1, n))
    return out[:bsz]


def _inproj_kernel(x_ref, g_ref, sc_ref, sh_ref, w_ref, ws_ref, o_ref, os_ref, h_ref):
    @pl.when(pl.program_id(2) == 0)
    def _():
        x = x_ref[...]
        y = x * _rms_scale(x) * g_ref[...]
        h = (y * (1.0 + sc_ref[...]) + sh_ref[...]).astype(BF16)
        h_ref[...] = h
        os_ref[...] = jnp.dot(h, ws_ref[...], preferred_element_type=F32)

    o_ref[...] = jnp.dot(h_ref[...], w_ref[...], preferred_element_type=F32).astype(BF16)


def _in_proj(x, g, sc, sh, w_main, n, w_small, bm, bn):
    bsz, s, d = x.shape
    ns = w_small.shape[1]
    return pl.pallas_call(
        _inproj_kernel,
        out_shape=(jax.ShapeDtypeStruct((bsz, s, n), BF16), jax.ShapeDtypeStruct((bsz, s, ns), F32)),
        grid=(bsz, s // bm, n // bn),
        in_specs=[pl.BlockSpec((None, bm, d), lambda b, i, j: (b, i, 0)),
                  pl.BlockSpec((1, d), lambda b, i, j: (0, 0)),
                  pl.BlockSpec((None, 1, d), lambda b, i, j: (b, 0, 0)),
                  pl.BlockSpec((None, 1, d), lambda b, i, j: (b, 0, 0)),
                  pl.BlockSpec((d, bn), lambda b, i, j: (0, j)),
                  pl.BlockSpec((d, ns), lambda b, i, j: (0, 0))],
        out_specs=(pl.BlockSpec((None, bm, bn), lambda b, i, j: (b, i, j)),
                   pl.BlockSpec((None, bm, ns), lambda b, i, j: (b, i, 0))),
        scratch_shapes=[pltpu.VMEM((bm, d), BF16)],
        compiler_params=_cparams(("arbitrary", "arbitrary", "arbitrary")),
        name="in_proj",
    )(x, g, sc, sh, w_main, w_small)


def _rglru_kernel(xl_ref, gate_ref, cw_ref, cb_ref, wa_ref, ba_ref, wx_ref, bx_ref, lam_ref, g_ref,
                  o_ref, halo_ref, hc_ref, a_scr, b_scr):
    bt, dl = xl_ref.shape
    nblk = dl // LRU_BLOCK

    @pl.when(pl.program_id(1) == 0)
    def _():
        halo_ref[...] = jnp.zeros_like(halo_ref)
        hc_ref[...] = jnp.zeros_like(hc_ref)

    xl = xl_ref[...].astype(F32)
    prev = halo_ref[...]
    halo_ref[...] = xl[bt - SUBLANES:, :]

    cw = cw_ref[...]
    kw = cw.shape[0]
    xc = cb_ref[...] + cw[kw - 1:kw, :] * xl
    for dlt in range(1, kw):
        xc = xc + cw[kw - 1 - dlt:kw - dlt, :] * _shift_rows(xl, prev, dlt)

    xcb = xc.astype(BF16)
    r_parts, i_parts = [], []
    for n in range(nblk):
        blk = xcb[:, n * LRU_BLOCK:(n + 1) * LRU_BLOCK]
        r_parts.append(jnp.dot(blk, wa_ref[n], preferred_element_type=F32))
        i_parts.append(jnp.dot(blk, wx_ref[n], preferred_element_type=F32))
    r = jax.nn.sigmoid(jnp.concatenate(r_parts, axis=1) + ba_ref[...])
    ig = jax.nn.sigmoid(jnp.concatenate(i_parts, axis=1) + bx_ref[...])

    z = -lam_ref[...]
    softplus = jnp.maximum(z, 0.0) + jnp.log(1.0 + jnp.exp(-jnp.abs(z)))
    log_a = (-LRU_C * softplus) * r
    a = jnp.exp(log_a)
    u = jnp.sqrt(1.0 - jnp.exp(2.0 * log_a)) * (ig * xc)

    rowmod = lax.broadcasted_iota(I32, (bt, dl), 0) % SUBLANES
    b = u
    for sft in (1, 2, 4):
        keep = rowmod >= sft
        a_prev = jnp.where(keep, pltpu.roll(a, sft, 0), 1.0)
        b_prev = jnp.where(keep, pltpu.roll(b, sft, 0), 0.0)
        b = a * b_prev + b
        a = a * a_prev
    a_scr[...] = a
    b_scr[...] = b

    def group(gi, carry):
        rows = pl.ds(pl.multiple_of(gi * SUBLANES, SUBLANES), SUBLANES)
        h = a_scr[rows, :] * carry + b_scr[rows, :]
        b_scr[rows, :] = h
        return jnp.broadcast_to(h[SUBLANES - 1:SUBLANES, :], (SUBLANES, dl))

    hc_ref[...] = lax.fori_loop(0, bt // SUBLANES, group, hc_ref[...])

    y = b_scr[...] * _gelu_tanh(gate_ref[...].astype(F32))
    o_ref[...] = (y * _rms_scale(y) * g_ref[...]).astype(BF16)


def _rglru(proj, x_off, cw, cb, wa, ba, wx, bx, lam, g, bt):
    bsz, s, _ = proj.shape
    dl = cw.shape[1]
    nblk = dl // LRU_BLOCK
    x_blk = x_off // dl
    vec = lambda: pl.BlockSpec((1, dl), lambda b, t: (0, 0))
    return pl.pallas_call(
        _rglru_kernel,
        out_shape=jax.ShapeDtypeStruct((bsz, s, dl), BF16),
        grid=(bsz, s // bt),
        in_specs=[pl.BlockSpec((None, bt, dl), lambda b, t: (b, t, x_blk)),
                  pl.BlockSpec((None, bt, dl), lambda b, t: (b, t, x_blk + 1)),
                  pl.BlockSpec((cw.shape[0], dl), lambda b, t: (0, 0)),
                  vec(),
                  pl.BlockSpec((nblk, LRU_BLOCK, LRU_BLOCK), lambda b, t: (0, 0, 0)),
                  vec(),
                  pl.BlockSpec((nblk, LRU_BLOCK, LRU_BLOCK), lambda b, t: (0, 0, 0)),
                  vec(), vec(), vec()],
        out_specs=pl.BlockSpec((None, bt, dl), lambda b, t: (b, t, 0)),
        scratch_shapes=[pltpu.VMEM((SUBLANES, dl), F32), pltpu.VMEM((SUBLANES, dl), F32),
                        pltpu.VMEM((bt, dl), F32), pltpu.VMEM((bt, dl), F32)],
        compiler_params=_cparams(("arbitrary", "arbitrary")),
        name="rglru",
    )(proj, proj, cw, cb, wa, ba, wx, bx, lam, g)


def _dsa_kernel(q_ref, *refs, k_sel, rep, n_qi):
    qi_refs = refs[:n_qi]
    (k_ref, v_ref, kd_ref, w_ref, g_ref, o_ref,
     key_scr, m_scr, l_scr, acc_scr, out_scr) = refs[n_qi:]
    bq = q_ref.shape[0]
    nkb_all, _, kb2 = kd_ref.shape
    kb_sz = kb2 // 2
    n_groups = q_ref.shape[1] // (rep * HEAD_DIM)
    qblk = pl.program_id(1)
    nkb = lax.div((qblk + 1) * bq + kb_sz - 1, kb_sz)
    qi_w = qi_refs[0].shape[1]
    n_pairs = n_qi * qi_w // LANES
    lane_tiles = kb_sz // LANES
    tile = lambda v: jnp.concatenate([v] * lane_tiles, axis=1)

    w = w_ref[...]
    w_off = IDX_DIM

    def score_block(kb, carry):
        kd = kd_ref[kb]
        acc = jnp.zeros((bq, kb_sz), F32)
        for p in range(n_pairs):
            piece, col = divmod(p * LANES, qi_w)
            d2 = jnp.dot(qi_refs[piece][:, col:col + LANES], kd, preferred_element_type=F32)
            we = w[:, w_off + 2 * p:w_off + 2 * p + 1]
            wo = w[:, w_off + 2 * p + 1:w_off + 2 * p + 2]
            acc = acc + jnp.maximum(d2[:, :kb_sz], 0.0) * we + jnp.maximum(d2[:, kb_sz:], 0.0) * wo
        bits = lax.bitcast_convert_type(acc, I32)
        key = jnp.where(bits < 0, bits ^ jnp.int32(0x7FFFFFFF), bits)
        kpos = kb * kb_sz + lax.broadcasted_iota(I32, (bq, kb_sz), 1)
        qpos = qblk * bq + lax.broadcasted_iota(I32, (bq, kb_sz), 0)
        key_scr[kb] = jnp.where(kpos <= qpos, key, jnp.int32(INT_MIN))
        return carry

    lax.fori_loop(0, nkb, score_block, 0)

    def count_lanes(hit_fn, rows):
        def body(kb, c):
            hit = hit_fn(kb)
            for t in range(lane_tiles):
                c = c + jnp.where(hit[:, t * LANES:(t + 1) * LANES], 1.0, 0.0)
            return c
        return lax.fori_loop(0, nkb, body, jnp.zeros((rows, LANES), F32))

    def lane_total(c):
        return jnp.broadcast_to(jnp.sum(c, axis=1, keepdims=True), c.shape)

    def count_rows(hit_fn, rows):
        return lane_total(count_lanes(hit_fn, rows))

    def bit_step(it, carry):
        bit = lax.shift_left(jnp.int32(1), 31 - it)
        cands, cnts = [], []
        for part, (ans, _) in enumerate(carry):
            rows = slice(part * SELECT_ROWS, (part + 1) * SELECT_ROWS)
            cand_u = ans | bit
            cand_rows = tile(cand_u ^ jnp.int32(INT_MIN))
            cands.append(cand_u)
            cnts.append(count_lanes(lambda kb, rows=rows, cand_rows=cand_rows: key_scr[kb, rows, :] >= cand_rows,
                                    SELECT_ROWS))
        out = []
        for (ans, n_ge), cand_u, lanes in zip(carry, cands, cnts):
            cnt = lane_total(lanes)
            ok = cnt >= k_sel
            out.append((jnp.where(ok, cand_u, ans), jnp.where(ok, cnt, n_ge)))
        return tuple(out)

    zero = jnp.zeros((SELECT_ROWS, LANES), I32)
    parts = lax.fori_loop(0, 32, bit_step, tuple((zero, zero.astype(F32)) for _ in range(bq // SELECT_ROWS)))
    ans = jnp.concatenate([p[0] for p in parts], axis=0)
    n_ge = jnp.concatenate([p[1] for p in parts], axis=0)
    thr = jnp.maximum(ans ^ jnp.int32(INT_MIN), jnp.int32(INT_MIN + 1))
    thr_rows = tile(thr)

    def count_ge(cand):
        cand_rows = tile(cand)
        return count_rows(lambda kb: key_scr[kb] >= cand_rows, bq)

    @pl.when(jnp.max(n_ge) > k_sel)
    def _():
        need = k_sel - count_ge(thr + 1)
        pos_bits = (nkb_all * kb_sz - 1).bit_length()

        def ties_before(pos):
            pos_rows = tile(pos)

            def hit(kb):
                kpos = kb * kb_sz + lax.broadcasted_iota(I32, (bq, kb_sz), 1)
                return (key_scr[kb] == thr_rows) & (kpos < pos_rows)
            return count_rows(hit, bq)

        def pos_step(it, last):
            cand = last | lax.shift_left(jnp.int32(1), pos_bits - 1 - it)
            return jnp.where(ties_before(cand) < need, cand, last)

        last_rows = tile(lax.fori_loop(0, pos_bits, pos_step, jnp.zeros((bq, LANES), I32)))

        def demote(kb, carry):
            key = key_scr[kb]
            kpos = kb * kb_sz + lax.broadcasted_iota(I32, (bq, kb_sz), 1)
            key_scr[kb] = jnp.where((key == thr_rows) & (kpos > last_rows), key - 1, key)
            return carry

        lax.fori_loop(0, nkb, demote, 0)

    c_exp = (HEAD_DIM ** -0.5) * LOG2_E
    for g in range(n_groups):
        qs = jnp.concatenate(
            [q_ref[:, (g * rep + r) * HEAD_DIM:(g * rep + r + 1) * HEAD_DIM] for r in range(rep)], axis=0)
        m_scr[...] = jnp.full_like(m_scr, NEG_BIG)
        l_scr[...] = jnp.zeros_like(l_scr)
        acc_scr[...] = jnp.zeros_like(acc_scr)

        def kv_block(kb, carry):
            rows = pl.ds(pl.multiple_of(kb * kb_sz, kb_sz), kb_sz)
            kblk = k_ref[rows, g * HEAD_DIM:(g + 1) * HEAD_DIM]
            vblk = v_ref[rows, g * HEAD_DIM:(g + 1) * HEAD_DIM]
            s = lax.dot_general(qs, kblk, (((1,), (1,)), ((), ())), preferred_element_type=F32)
            sel = key_scr[kb] >= thr_rows
            s = jnp.where(jnp.concatenate([sel] * rep, axis=0), s, NEG_BIG)
            m_prev = m_scr[...]
            m_new = jnp.maximum(m_prev, jnp.max(s, axis=1, keepdims=True))
            alpha = jnp.exp2((m_prev - m_new) * c_exp)
            p = jnp.exp2((s - tile(m_new)) * c_exp)
            l_scr[...] = alpha * l_scr[...] + jnp.sum(p, axis=1, keepdims=True)
            acc_scr[...] = alpha * acc_scr[...] + jnp.dot(p.astype(BF16), vblk, preferred_element_type=F32)
            m_scr[...] = m_new
            return carry

        lax.fori_loop(0, nkb, kv_block, 0)
        o = acc_scr[...] / l_scr[...]
        for r in range(rep):
            out_scr[:, (g * rep + r) * HEAD_DIM:(g * rep + r + 1) * HEAD_DIM] = o[r * bq:(r + 1) * bq, :]

    y = out_scr[...]
    o_ref[...] = (y * _rms_scale(y) * g_ref[...]).astype(BF16)


def _dsa(proj, q_off, qi_off, k_off, small, kd, g, d_attn, d_kv, bq):
    bsz, s, _ = proj.shape
    nkb_all, _, kb2 = kd.shape[1:]
    rep = d_attn // d_kv
    k_sel = min(TOPK_MAX, s // 4)
    d_qi = N_IDX_HEADS * IDX_DIM
    qi_w = math.gcd(qi_off, d_qi)
    n_qi = d_qi // qi_w
    q_blk = q_off // d_attn
    k_blk = k_off // d_kv
    kern = functools.partial(_dsa_kernel, k_sel=k_sel, rep=rep, n_qi=n_qi)
    qi_specs = [pl.BlockSpec((None, bq, qi_w), lambda b, i, t=t: (b, i, qi_off // qi_w + t)) for t in range(n_qi)]
    return pl.pallas_call(
        kern,
        out_shape=jax.ShapeDtypeStruct((bsz, s, d_attn), BF16),
        grid=(bsz, s // bq),
        in_specs=[pl.BlockSpec((None, bq, d_attn), lambda b, i: (b, i, q_blk))] + qi_specs + [
                  pl.BlockSpec((None, s, d_kv), lambda b, i: (b, 0, k_blk)),
                  pl.BlockSpec((None, s, d_kv), lambda b, i: (b, 0, k_blk + 1)),
                  pl.BlockSpec((None, nkb_all, LANES, kb2), lambda b, i: (b, 0, 0, 0)),
                  pl.BlockSpec((None, bq, LANES), lambda b, i: (b, i, 0)),
                  pl.BlockSpec((1, d_attn), lambda b, i: (0, 0))],
        out_specs=pl.BlockSpec((None, bq, d_attn), lambda b, i: (b, i, 0)),
        scratch_shapes=[pltpu.VMEM((nkb_all, bq, kb2 // 2), I32),
                        pltpu.VMEM((rep * bq, LANES), F32),
                        pltpu.VMEM((rep * bq, LANES), F32),
                        pltpu.VMEM((rep * bq, HEAD_DIM), F32),
                        pltpu.VMEM((bq, d_attn), F32)],
        compiler_params=_cparams(("arbitrary", "arbitrary")),
        name="dsa",
    )(proj, *([proj] * n_qi), proj, proj, kd, small, g)


def _outproj_kernel(a1_ref, a2_ref, w_ref, x_ref, gpost_ref, gt_ref, gpre_ref, sc_ref, sh_ref,
                    x1_ref, h2_ref, mix_ref):
    j = pl.program_id(2)
    n_tiles, bm, tn = mix_ref.shape
    k1 = a1_ref.shape[1]
    mix_ref[j] = (jnp.dot(a1_ref[...], w_ref[0:k1, :], preferred_element_type=F32)
                  + jnp.dot(a2_ref[...], w_ref[k1:, :], preferred_element_type=F32))

    @pl.when(j == n_tiles - 1)
    def _():
        post = gt_ref[...] * gpost_ref[...]
        pre = gpre_ref[...] * (1.0 + sc_ref[...])
        sh = sh_ref[...]

        def rows_out(r, carry):
            rows = pl.ds(pl.multiple_of(r * SUBLANES, SUBLANES), SUBLANES)
            m = jnp.concatenate([mix_ref[t, rows, :] for t in range(n_tiles)], axis=1)
            x1 = x_ref[rows, :] + post * (m * _rms_scale(m))
            x1_ref[rows, :] = x1
            h2_ref[rows, :] = ((x1 * _rms_scale(x1)) * pre + sh).astype(BF16)
            return carry

        lax.fori_loop(0, bm // SUBLANES, rows_out, 0, unroll=8)


def _out_proj(a1, a2, w_tiles, x, gpost, gt, gpre, sc, sh, bm):
    bsz, s, d = x.shape
    k1, k2 = a1.shape[2], a2.shape[2]
    n_tiles, _, tn = w_tiles.shape
    vec = lambda: pl.BlockSpec((1, d), lambda b, i, j: (0, 0))
    bvec = lambda: pl.BlockSpec((None, 1, d), lambda b, i, j: (b, 0, 0))
    big = lambda: pl.BlockSpec((None, bm, d), lambda b, i, j: (b, i, 0), pipeline_mode=pl.Buffered(1))
    return pl.pallas_call(
        _outproj_kernel,
        out_shape=(jax.ShapeDtypeStruct((bsz, s, d), F32), jax.ShapeDtypeStruct((bsz, s, d), BF16)),
        grid=(bsz, s // bm, n_tiles),
        in_specs=[pl.BlockSpec((None, bm, k1), lambda b, i, j: (b, i, 0)),
                  pl.BlockSpec((None, bm, k2), lambda b, i, j: (b, i, 0)),
                  pl.BlockSpec((None, k1 + k2, tn), lambda b, i, j: (j, 0, 0)),
                  big(), vec(), bvec(), vec(), bvec(), bvec()],
        out_specs=(big(), big()),
        scratch_shapes=[pltpu.VMEM((n_tiles, bm, tn), F32)],
        compiler_params=_cparams(("arbitrary", "arbitrary", "arbitrary")),
        name="out_proj",
    )(a1, a2, w_tiles, x, gpost, gt, gpre, sc, sh)


FFN_ROW_CHUNK = 64


def _ffn_kernel(h_ref, wg_ref, wu_ref, cwg_ref, cwu_ref, cbg_ref, cbu_ref, wo_ref, x_ref, g_ref, gt_ref,
                o_ref, acc_ref, u_even, u_odd, act_even, act_odd, carry_ref):
    i = pl.program_id(1)
    s = pl.program_id(2)
    nj = pl.num_programs(2) - 2
    bm, d = h_ref.shape
    tf = wo_ref.shape[0]
    tile = jnp.minimum(s, nj - 1)

    @pl.when(s == 0)
    def _():
        acc_ref[...] = jnp.zeros_like(acc_ref)
        act_even[...] = jnp.zeros_like(act_even)
        u_odd[...] = jnp.zeros_like(u_odd)

    @pl.when(i == 0)
    def _():
        carry_ref[tile] = jnp.zeros(carry_ref.shape[1:], F32)

    def step(u_new, u_last, act_new, act_last):
        live = (s > 0).astype(F32)
        cwg, cwu, cbg, cbu = cwg_ref[...], cwu_ref[...], cbg_ref[...], cbu_ref[...]

        def conv(r0, c0, cw, cb):
            cols = slice(c0, c0 + tf)
            return (cb + cw[2:3, :] * u_last[r0:r0 + FFN_ROW_CHUNK, cols]
                    + cw[1:2, :] * u_last[r0 - 1:r0 - 1 + FFN_ROW_CHUNK, cols]
                    + cw[0:1, :] * u_last[r0 - 2:r0 - 2 + FFN_ROW_CHUNK, cols])

        def finish_rows(c, after):
            bits = lax.bitcast_convert_type(after[0:1, 0:tf], jnp.uint32)
            zero = lax.shift_right_logical(lax.shift_right_logical(bits, jnp.uint32(16)), jnp.uint32(16)).astype(F32)
            r0 = SUBLANES + c * FFN_ROW_CHUNK
            gate = conv(r0, 0, cwg, cbg + zero)
            up = conv(r0, tf, cwu, cbu + zero)
            act_new[c * FFN_ROW_CHUNK:(c + 1) * FFN_ROW_CHUNK, :] = (_gelu_tanh(gate) * up * live).astype(BF16)

        n_chunks = bm // FFN_ROW_CHUNK
        nw = d // n_chunks
        act_l = act_last[...]
        for c in range(n_chunks):
            cols = slice(c * nw, (c + 1) * nw)
            piece = jnp.dot(act_l, wo_ref[:, cols], preferred_element_type=F32)
            acc_ref[:, cols] += piece
            finish_rows(c, piece)

        w_gu = jnp.concatenate([wg_ref[...], wu_ref[...]], axis=1)
        u = jnp.dot(h_ref[...], w_gu, preferred_element_type=F32)
        u_new[0:SUBLANES, :] = carry_ref[tile]
        u_new[SUBLANES:, :] = u
        carry_ref[tile] = u[bm - SUBLANES:, :]

    parity = lax.rem(s, 2)

    @pl.when(parity == 0)
    def _():
        step(u_even, u_odd, act_odd, act_even)

    @pl.when(parity == 1)
    def _():
        step(u_odd, u_even, act_even, act_odd)

    @pl.when(s == nj + 1)
    def _():
        scale = gt_ref[...] * g_ref[...]

        def rows_out(r, carry):
            rows = pl.ds(pl.multiple_of(r * SUBLANES, SUBLANES), SUBLANES)
            f = acc_ref[rows, :]
            o_ref[rows, :] = x_ref[rows, :] + scale * (f * _rms_scale(f))
            return carry

        lax.fori_loop(0, bm // SUBLANES, rows_out, 0, unroll=8)


def _conv_ffn(h, w_in, cw, cb, wo, x, g, gt, bm, tf):
    bsz, s, d = x.shape
    nj = wo.shape[0] // tf
    kw = cw.shape[0]
    up_t = lambda j: jnp.minimum(j, nj - 1)
    ew_t = lambda j: jnp.clip(j - 1, 0, nj - 1)
    dn_t = lambda j: jnp.clip(j - 2, 0, nj - 1)
    big = lambda: pl.BlockSpec((None, bm, d), lambda b, i, j: (b, i, 0), pipeline_mode=pl.Buffered(1))
    return pl.pallas_call(
        _ffn_kernel,
        out_shape=jax.ShapeDtypeStruct((bsz, s, d), F32),
        grid=(bsz, s // bm, nj + 2),
        in_specs=[big(),
                  pl.BlockSpec((None, d, tf), lambda b, i, j: (up_t(j), 0, 0)),
                  pl.BlockSpec((None, d, tf), lambda b, i, j: (nj + up_t(j), 0, 0)),
                  pl.BlockSpec((kw, tf), lambda b, i, j: (0, ew_t(j))),
                  pl.BlockSpec((kw, tf), lambda b, i, j: (0, nj + ew_t(j))),
                  pl.BlockSpec((1, tf), lambda b, i, j: (0, ew_t(j))),
                  pl.BlockSpec((1, tf), lambda b, i, j: (0, nj + ew_t(j))),
                  pl.BlockSpec((tf, d), lambda b, i, j: (dn_t(j), 0)),
                  big(),
                  pl.BlockSpec((1, d), lambda b, i, j: (0, 0)),
                  pl.BlockSpec((None, 1, d), lambda b, i, j: (b, 0, 0))],
        out_specs=big(),
        scratch_shapes=[pltpu.VMEM((bm, d), F32),
                        pltpu.VMEM((SUBLANES + bm, 2 * tf), F32),
                        pltpu.VMEM((SUBLANES + bm, 2 * tf), F32),
                        pltpu.VMEM((bm, tf), BF16),
                        pltpu.VMEM((bm, tf), BF16),
                        pltpu.VMEM((nj, SUBLANES, 2 * tf), F32)],
        compiler_params=_cparams(("arbitrary", "arbitrary", "arbitrary")),
        name="conv_ffn",
    )(h, w_in, w_in, cw, cw, cb, cb, wo, x, g, gt)


def _layer(x, c, w_ada, b_ada, g_pre_mix, g_post_mix, g_pre_ffn, g_post_ffn, w_in,
           conv_lru_w, conv_lru_b, w_rg_a, b_rg_a, w_rg_x, b_rg_x, lru_lambda,
           g_grp_lru, g_grp_attn, w_out, w_ffn_in, conv_ffn_w, conv_ffn_b, w_ffn_out,
           *, bm_in, bn_in, bt_lru, bq, kb_sz, bm_out, tn_out, bm_ffn, tf):
    bsz, s, d = x.shape
    d_lru = w_rg_a.shape[0] * LRU_BLOCK
    d_attn = g_grp_attn.shape[0]
    d_qi = N_IDX_HEADS * IDX_DIM
    d_kv = (w_in.shape[1] - 2 * d_lru - d_attn - d_qi - IDX_DIM - N_IDX_HEADS) // 2
    row = lambda v: v.reshape(1, -1)

    mod = _modulation(c, w_ada, b_ada)
    sh1, sc1, gt1, sh2, sc2, gt2 = [m.reshape(bsz, 1, d) for m in jnp.split(mod, 6, axis=-1)]

    q_off = 2 * d_lru
    k_off = q_off + d_attn
    qi_off = k_off + 2 * d_kv
    n_main = qi_off + d_qi
    n_small = w_in.shape[1] - n_main
    w_in_b = w_in.astype(BF16)
    w_small = jnp.concatenate([w_in[:, n_main:], jnp.zeros((d, LANES - n_small), F32)], axis=1).astype(BF16)

    proj, small = _in_proj(x, row(g_pre_mix), sc1, sh1, w_in_b, n_main, w_small, bm_in, bn_in)

    y_lru = _rglru(proj, 0, conv_lru_w, row(conv_lru_b), w_rg_a.astype(BF16), row(b_rg_a),
                   w_rg_x.astype(BF16), row(b_rg_x), row(lru_lambda), row(g_grp_lru), bt_lru)

    kt = jnp.swapaxes(small[:, :, :IDX_DIM].astype(BF16).reshape(bsz, s // kb_sz, kb_sz, IDX_DIM), 2, 3)
    zz = jnp.zeros_like(kt)
    kd = jnp.concatenate([jnp.concatenate([kt, zz], axis=3), jnp.concatenate([zz, kt], axis=3)], axis=2)
    y_attn = _dsa(proj, q_off, qi_off, k_off, small, kd, row(g_grp_attn), d_attn, d_kv, bq)

    w_out_t = jnp.transpose(w_out.astype(BF16).reshape(w_out.shape[0], -1, tn_out), (1, 0, 2))
    x1, h2 = _out_proj(y_lru, y_attn, w_out_t, x, row(g_post_mix), gt1, row(g_pre_ffn), sc2, sh2, bm_out)

    w_up = jnp.transpose(w_ffn_in.astype(BF16).reshape(d, -1, tf), (1, 0, 2))
    return _conv_ffn(h2, w_up, conv_ffn_w, row(conv_ffn_b), w_ffn_out.astype(BF16),
                     x1, row(g_post_ffn), gt2, bm_ffn, tf)


_TILES = dict(bm_in=512, bn_in=1024, bt_lru=256, bq=256, kb_sz=512, bm_out=512, tn_out=512, bm_ffn=512, tf=256)


def kernel(x, c, w_ada, b_ada, g_pre_mix, g_post_mix, g_pre_ffn, g_post_ffn, w_in, conv_lru_w, conv_lru_b,
           w_rg_a, b_rg_a, w_rg_x, b_rg_x, lru_lambda, g_grp_lru, g_grp_attn, w_out, w_ffn_in,
           conv_ffn_w, conv_ffn_b, w_ffn_out):
    for l in range(w_ada.shape[0]):
        x = _layer(x, c, w_ada[l], b_ada[l], g_pre_mix[l], g_post_mix[l], g_pre_ffn[l], g_post_ffn[l],
                   w_in[l], conv_lru_w[l], conv_lru_b[l], w_rg_a[l], b_rg_a[l], w_rg_x[l], b_rg_x[l],
                   lru_lambda[l], g_grp_lru[l], g_grp_attn[l], w_out[l], w_ffn_in[l], conv_ffn_w[l],
                   conv_ffn_b[l], w_ffn_out[l], **_TILES)
    return x
```

```python
import functools
import math

import jax
import jax.numpy as jnp
from jax import lax
from jax.experimental import pallas as pl
from jax.experimental.pallas import tpu as pltpu

F32 = jnp.float32
BF16 = jnp.bfloat16
I32 = jnp.int32

RMS_EPS = 1e-6
LRU_C = 8.0
LRU_BLOCK = 128
HEAD_DIM = 128
N_KV_HEADS = 4
N_IDX_HEADS = 32
IDX_DIM = 64
TOPK_MAX = 256

LANES = 128
SUBLANES = 8
VMEM_LIMIT = 56 * 1024 * 1024

ATTN_HEADS_PER_SWEEP = 2
COUNT_CHAINS = 2
INT_MIN = -(2 ** 31)
NEG_BIG = -1e30
LOG2_E = 1.4426950408889634


def _cparams(sem, flags=None):
    return pltpu.CompilerParams(dimension_semantics=sem, vmem_limit_bytes=VMEM_LIMIT, flags=flags)


def _rms_scale(v):
    return lax.rsqrt(jnp.mean(v * v, axis=-1, keepdims=True) + RMS_EPS)


def _gelu_tanh(v):
    inner = v * (1.0 + 0.044715 * (v * v))
    return v / (1.0 + jnp.exp((-2.0 * 0.7978845608028654) * inner))


def _shift_rows(cur, prev8, d):
    rolled = pltpu.roll(cur, d, 0)
    head = pltpu.roll(prev8, d, 0)
    row = lax.broadcasted_iota(I32, (SUBLANES, cur.shape[1]), 0)
    top = jnp.where(row < d, head, rolled[0:SUBLANES])
    return jnp.concatenate([top, rolled[SUBLANES:]], axis=0)


def _mod_kernel(c_ref, w_ref, b_ref, o_ref):
    c = c_ref[...]
    s = c * jax.nn.sigmoid(c)
    o_ref[...] = jnp.dot(s, w_ref[...], preferred_element_type=F32,
                         precision=lax.Precision.HIGHEST) + b_ref[...]


def _modulation(c, w_ada, b_ada, tn=512):
    bsz, d = c.shape
    n = w_ada.shape[1]
    rows = -(-bsz // SUBLANES) * SUBLANES
    c_pad = jnp.zeros((rows, d), F32).at[:bsz].set(c)
    out = pl.pallas_call(
        _mod_kernel,
        out_shape=jax.ShapeDtypeStruct((rows, n), F32),
        grid=(n // tn,),
        in_specs=[pl.BlockSpec((rows, d), lambda j: (0, 0)),
                  pl.BlockSpec((d, tn), lambda j: (0, j)),
                  pl.BlockSpec((1, tn), lambda j: (0, j))],
        out_specs=pl.BlockSpec((rows, tn), lambda j: (0, j)),
        compiler_params=_cparams(("arbitrary",)),
        name="modulation",
    )(c_pad, w_ada, b_ada.reshape(1, n))
    return out[:bsz]


def _inproj_kernel(x_ref, g_ref, sc_ref, sh_ref, w_ref, ws_ref, o_ref, os_ref, h_ref):
    @pl.when(pl.program_id(2) == 0)
    def _():
        x = x_ref[...]
        y = x * _rms_scale(x) * g_ref[...]
        h = (y * (1.0 + sc_ref[...]) + sh_ref[...]).astype(BF16)
        h_ref[...] = h
        os_ref[...] = jnp.dot(h, ws_ref[...], preferred_element_type=F32)

    o_ref[...] = jnp.dot(h_ref[...], w_ref[...], preferred_element_type=F32).astype(BF16)


def _in_proj(x, g, sc, sh, w_main, n, w_small, bm, bn):
    bsz, s, d = x.shape
    ns = w_small.shape[1]
    return pl.pallas_call(
        _inproj_kernel,
        out_shape=(jax.ShapeDtypeStruct((bsz, s, n), BF16), jax.ShapeDtypeStruct((bsz, s, ns), F32)),
        grid=(bsz, s // bm, n // bn),
        in_specs=[pl.BlockSpec((None, bm, d), lambda b, i, j: (b, i, 0)),
                  pl.BlockSpec((1, d), lambda b, i, j: (0, 0)),
                  pl.BlockSpec((None, 1, d), lambda b, i, j: (b, 0, 0)),
                  pl.BlockSpec((None, 1, d), lambda b, i, j: (b, 0, 0)),
                  pl.BlockSpec((d, bn), lambda b, i, j: (0, j)),
                  pl.BlockSpec((d, ns), lambda b, i, j: (0, 0))],
        out_specs=(pl.BlockSpec((None, bm, bn), lambda b, i, j: (b, i, j)),
                   pl.BlockSpec((None, bm, ns), lambda b, i, j: (b, i, 0))),
        scratch_shapes=[pltpu.VMEM((bm, d), BF16)],
        compiler_params=_cparams(("arbitrary", "arbitrary", "arbitrary")),
        name="in_proj",
    )(x, g, sc, sh, w_main, w_small)


def _rglru_kernel(xl_ref, gate_ref, cw_ref, cb_ref, wa_ref, ba_ref, wx_ref, bx_ref, lam_ref, g_ref,
                  o_ref, halo_ref, hc_ref, a_scr, b_scr):
    bt, dl = xl_ref.shape
    nblk = dl // LRU_BLOCK

    @pl.when(pl.program_id(1) == 0)
    def _():
        halo_ref[...] = jnp.zeros_like(halo_ref)
        hc_ref[...] = jnp.zeros_like(hc_ref)

    xl = xl_ref[...].astype(F32)
    prev = halo_ref[...]
    halo_ref[...] = xl[bt - SUBLANES:, :]

    cw = cw_ref[...]
    kw = cw.shape[0]
    xc = cb_ref[...] + cw[kw - 1:kw, :] * xl
    for dlt in range(1, kw):
        xc = xc + cw[kw - 1 - dlt:kw - dlt, :] * _shift_rows(xl, prev, dlt)

    xcb = xc.astype(BF16)
    r_parts, i_parts = [], []
    for n in range(nblk):
        blk = xcb[:, n * LRU_BLOCK:(n + 1) * LRU_BLOCK]
        r_parts.append(jnp.dot(blk, wa_ref[n], preferred_element_type=F32))
        i_parts.append(jnp.dot(blk, wx_ref[n], preferred_element_type=F32))
    r = jax.nn.sigmoid(jnp.concatenate(r_parts, axis=1) + ba_ref[...])
    ig = jax.nn.sigmoid(jnp.concatenate(i_parts, axis=1) + bx_ref[...])

    z = -lam_ref[...]
    softplus = jnp.maximum(z, 0.0) + jnp.log(1.0 + jnp.exp(-jnp.abs(z)))
    log_a = (-LRU_C * softplus) * r
    a = jnp.exp(log_a)
    u = jnp.sqrt(1.0 - jnp.exp(2.0 * log_a)) * (ig * xc)

    rowmod = lax.broadcasted_iota(I32, (bt, dl), 0) % SUBLANES
    b = u
    for sft in (1, 2, 4):
        keep = rowmod >= sft
        a_prev = jnp.where(keep, pltpu.roll(a, sft, 0), 1.0)
        b_prev = jnp.where(keep, pltpu.roll(b, sft, 0), 0.0)
        b = a * b_prev + b
        a = a * a_prev
    a_scr[...] = a
    b_scr[...] = b

    def group(gi, carry):
        rows = pl.ds(pl.multiple_of(gi * SUBLANES, SUBLANES), SUBLANES)
        h = a_scr[rows, :] * carry + b_scr[rows, :]
        b_scr[rows, :] = h
        return jnp.broadcast_to(h[SUBLANES - 1:SUBLANES, :], (SUBLANES, dl))

    hc_ref[...] = lax.fori_loop(0, bt // SUBLANES, group, hc_ref[...])

    y = b_scr[...] * _gelu_tanh(gate_ref[...].astype(F32))
    o_ref[...] = (y * _rms_scale(y) * g_ref[...]).astype(BF16)


def _rglru(proj, x_off, cw, cb, wa, ba, wx, bx, lam, g, bt):
    bsz, s, _ = proj.shape
    dl = cw.shape[1]
    nblk = dl // LRU_BLOCK
    x_blk = x_off // dl
    vec = lambda: pl.BlockSpec((1, dl), lambda b, t: (0, 0))
    return pl.pallas_call(
        _rglru_kernel,
        out_shape=jax.ShapeDtypeStruct((bsz, s, dl), BF16),
        grid=(bsz, s // bt),
        in_specs=[pl.BlockSpec((None, bt, dl), lambda b, t: (b, t, x_blk)),
                  pl.BlockSpec((None, bt, dl), lambda b, t: (b, t, x_blk + 1)),
                  pl.BlockSpec((cw.shape[0], dl), lambda b, t: (0, 0)),
                  vec(),
                  pl.BlockSpec((nblk, LRU_BLOCK, LRU_BLOCK), lambda b, t: (0, 0, 0)),
                  vec(),
                  pl.BlockSpec((nblk, LRU_BLOCK, LRU_BLOCK), lambda b, t: (0, 0, 0)),
                  vec(), vec(), vec()],
        out_specs=pl.BlockSpec((None, bt, dl), lambda b, t: (b, t, 0)),
        scratch_shapes=[pltpu.VMEM((SUBLANES, dl), F32), pltpu.VMEM((SUBLANES, dl), F32),
                        pltpu.VMEM((bt, dl), F32), pltpu.VMEM((bt, dl), F32)],
        compiler_params=_cparams(("arbitrary", "arbitrary")),
        name="rglru",
    )(proj, proj, cw, cb, wa, ba, wx, bx, lam, g)


def _dsa_kernel(q_ref, *refs, k_sel, rep, n_qi):
    qi_refs = refs[:n_qi]
    (k_ref, v_ref, kd_ref, w_ref, g_ref, o_ref,
     key_scr, bias_scr, m_scr, l_scr, acc_scr, out_scr) = refs[n_qi:]
    bq = q_ref.shape[0]
    nkb_all, kb2, _ = kd_ref.shape
    kb_sz = kb2 // 2
    n_groups = q_ref.shape[1] // (rep * HEAD_DIM)
    qblk = pl.program_id(1)
    nkb = lax.div((qblk + 1) * bq + kb_sz - 1, kb_sz)
    qi_w = qi_refs[0].shape[1]
    n_pairs = n_qi * qi_w // LANES
    lane_tiles = kb_sz // LANES
    tile = lambda v: jnp.concatenate([v] * lane_tiles, axis=1)
    down = lambda v: jnp.concatenate([v] * (kb_sz // SUBLANES), axis=0)

    w_t = jnp.transpose(w_ref[...])
    w_off = IDX_DIM

    def score_block(kb, carry):
        kd = kd_ref[kb]
        acc = jnp.zeros((kb_sz, bq), F32)
        for p in range(n_pairs):
            piece, col = divmod(p * LANES, qi_w)
            d2 = lax.dot_general(kd, qi_refs[piece][:, col:col + LANES], (((1,), (1,)), ((), ())),
                                 preferred_element_type=F32)
            we = w_t[w_off + 2 * p:w_off + 2 * p + 1, :]
            wo = w_t[w_off + 2 * p + 1:w_off + 2 * p + 2, :]
            acc = acc + jnp.maximum(d2[:kb_sz, :], 0.0) * we + jnp.maximum(d2[kb_sz:, :], 0.0) * wo
        bits = lax.bitcast_convert_type(acc, I32)
        key = jnp.where(bits < 0, bits ^ jnp.int32(0x7FFFFFFF), bits)
        kpos = kb * kb_sz + lax.broadcasted_iota(I32, (kb_sz, bq), 0)
        qpos = qblk * bq + lax.broadcasted_iota(I32, (kb_sz, bq), 1)
        key_scr[kb] = jnp.where(kpos <= qpos, key, jnp.int32(INT_MIN))
        return carry

    lax.fori_loop(0, nkb, score_block, 0)

    def count_keys(hit_fn):
        lanes = COUNT_CHAINS

        def body(kb, c):
            ones = jnp.where(hit_fn(kb), 1.0, 0.0)
            return c + jnp.sum(ones.reshape(lanes, kb_sz // (lanes * SUBLANES), SUBLANES, bq), axis=1)
        c = lax.fori_loop(0, nkb, body, jnp.zeros((lanes, SUBLANES, bq), F32))
        return jnp.broadcast_to(jnp.sum(c.reshape(lanes * SUBLANES, bq), axis=0, keepdims=True), (SUBLANES, bq))

    def count_ge(cand):
        cand_rows = down(cand)
        return count_keys(lambda kb: key_scr[kb] >= cand_rows)

    def bit_step(it, carry):
        ans, n_ge = carry
        cand_u = ans | lax.shift_left(jnp.int32(1), 31 - it)
        cnt = count_ge(cand_u ^ jnp.int32(INT_MIN))
        ok = cnt >= k_sel
        return jnp.where(ok, cand_u, ans), jnp.where(ok, cnt, n_ge)

    ans, n_ge = lax.fori_loop(0, 32, bit_step,
                              (jnp.zeros((SUBLANES, bq), I32), jnp.zeros((SUBLANES, bq), F32)))
    thr = jnp.maximum(ans ^ jnp.int32(INT_MIN), jnp.int32(INT_MIN + 1))
    thr_rows = down(thr)

    @pl.when(jnp.max(n_ge) > k_sel)
    def _():
        need = k_sel - count_ge(thr + 1)
        pos_bits = (nkb_all * kb_sz - 1).bit_length()

        def ties_before(pos):
            pos_rows = down(pos)

            def hit(kb):
                kpos = kb * kb_sz + lax.broadcasted_iota(I32, (kb_sz, bq), 0)
                return (key_scr[kb] == thr_rows) & (kpos < pos_rows)
            return count_keys(hit)

        def pos_step(it, last):
            cand = last | lax.shift_left(jnp.int32(1), pos_bits - 1 - it)
            return jnp.where(ties_before(cand) < need, cand, last)

        last_rows = down(lax.fori_loop(0, pos_bits, pos_step, jnp.zeros((SUBLANES, bq), I32)))

        def demote(kb, carry):
            key = key_scr[kb]
            kpos = kb * kb_sz + lax.broadcasted_iota(I32, (kb_sz, bq), 0)
            key_scr[kb] = jnp.where((key == thr_rows) & (kpos > last_rows), key - 1, key)
            return carry

        lax.fori_loop(0, nkb, demote, 0)

    def to_bias(kb, carry):
        bias = jnp.where(key_scr[kb] >= thr_rows, 0.0, NEG_BIG).astype(F32)
        bias_scr[kb] = jnp.transpose(bias)
        return carry

    lax.fori_loop(0, nkb, to_bias, 0)

    c_exp = (HEAD_DIM ** -0.5) * LOG2_E
    sweep = m_scr.shape[0]
    for g0 in range(0, n_groups, sweep):
        heads = range(g0, g0 + sweep)
        qss = [jnp.concatenate(
            [q_ref[:, (g * rep + r) * HEAD_DIM:(g * rep + r + 1) * HEAD_DIM] for r in range(rep)], axis=0)
            for g in heads]
        m_scr[...] = jnp.full_like(m_scr, NEG_BIG)
        l_scr[...] = jnp.zeros_like(l_scr)
        acc_scr[...] = jnp.zeros_like(acc_scr)

        def kv_block(kb, carry):
            rows = pl.ds(pl.multiple_of(kb * kb_sz, kb_sz), kb_sz)
            bias = jnp.concatenate([bias_scr[kb]] * rep, axis=0)
            logits = [lax.dot_general(qs, k_ref[rows, g * HEAD_DIM:(g + 1) * HEAD_DIM],
                                      (((1,), (1,)), ((), ())), preferred_element_type=F32)
                      for qs, g in zip(qss, heads)]
            for slot, (s, g) in enumerate(zip(logits, heads)):
                s = s + bias
                m_prev = m_scr[slot]
                m_new = jnp.maximum(m_prev, jnp.max(s, axis=1, keepdims=True))
                alpha = jnp.exp2((m_prev - m_new) * c_exp)
                p = jnp.exp2((s - tile(m_new)) * c_exp)
                l_scr[slot] = alpha * l_scr[slot] + jnp.sum(p, axis=1, keepdims=True)
                acc_scr[slot] = alpha * acc_scr[slot] + jnp.dot(
                    p.astype(BF16), v_ref[rows, g * HEAD_DIM:(g + 1) * HEAD_DIM], preferred_element_type=F32)
                m_scr[slot] = m_new
            return carry

        lax.fori_loop(0, nkb, kv_block, 0)
        for slot, g in enumerate(heads):
            o = acc_scr[slot] / l_scr[slot]
            for r in range(rep):
                out_scr[:, (g * rep + r) * HEAD_DIM:(g * rep + r + 1) * HEAD_DIM] = o[r * bq:(r + 1) * bq, :]

    y = out_scr[...]
    o_ref[...] = (y * _rms_scale(y) * g_ref[...]).astype(BF16)


def _dsa(proj, q_off, qi_off, k_off, small, kd, g, d_attn, d_kv, bq):
    bsz, s, _ = proj.shape
    nkb_all, kb2, _ = kd.shape[1:]
    rep = d_attn // d_kv
    k_sel = min(TOPK_MAX, s // 4)
    d_qi = N_IDX_HEADS * IDX_DIM
    qi_w = math.gcd(qi_off, d_qi)
    n_qi = d_qi // qi_w
    q_blk = q_off // d_attn
    k_blk = k_off // d_kv
    kern = functools.partial(_dsa_kernel, k_sel=k_sel, rep=rep, n_qi=n_qi)
    qi_specs = [pl.BlockSpec((None, bq, qi_w), lambda b, i, t=t: (b, i, qi_off // qi_w + t)) for t in range(n_qi)]
    return pl.pallas_call(
        kern,
        out_shape=jax.ShapeDtypeStruct((bsz, s, d_attn), BF16),
        grid=(bsz, s // bq),
        in_specs=[pl.BlockSpec((None, bq, d_attn), lambda b, i: (b, i, q_blk))] + qi_specs + [
                  pl.BlockSpec((None, s, d_kv), lambda b, i: (b, 0, k_blk)),
                  pl.BlockSpec((None, s, d_kv), lambda b, i: (b, 0, k_blk + 1)),
                  pl.BlockSpec((None, nkb_all, kb2, LANES), lambda b, i: (b, 0, 0, 0)),
                  pl.BlockSpec((None, bq, LANES), lambda b, i: (b, i, 0)),
                  pl.BlockSpec((1, d_attn), lambda b, i: (0, 0))],
        out_specs=pl.BlockSpec((None, bq, d_attn), lambda b, i: (b, i, 0)),
        scratch_shapes=[pltpu.VMEM((nkb_all, kb2 // 2, bq), I32),
                        pltpu.VMEM((nkb_all, bq, kb2 // 2), F32),
                        pltpu.VMEM((ATTN_HEADS_PER_SWEEP, rep * bq, LANES), F32),
                        pltpu.VMEM((ATTN_HEADS_PER_SWEEP, rep * bq, LANES), F32),
                        pltpu.VMEM((ATTN_HEADS_PER_SWEEP, rep * bq, HEAD_DIM), F32),
                        pltpu.VMEM((bq, d_attn), F32)],
        compiler_params=_cparams(("arbitrary", "arbitrary")),
        name="dsa",
    )(proj, *([proj] * n_qi), proj, proj, kd, small, g)


def _outproj_kernel(a1_ref, a2_ref, w_ref, x_ref, gpost_ref, gt_ref, gpre_ref, sc_ref, sh_ref,
                    x1_ref, h2_ref, mix_ref):
    j = pl.program_id(2)
    n_tiles, bm, tn = mix_ref.shape
    k1 = a1_ref.shape[1]
    mix_ref[j] = (jnp.dot(a1_ref[...], w_ref[0:k1, :], preferred_element_type=F32)
                  + jnp.dot(a2_ref[...], w_ref[k1:, :], preferred_element_type=F32))

    @pl.when(j == n_tiles - 1)
    def _():
        post = gt_ref[...] * gpost_ref[...]
        pre = gpre_ref[...] * (1.0 + sc_ref[...])
        sh = sh_ref[...]

        def rows_out(r, carry):
            rows = pl.ds(pl.multiple_of(r * SUBLANES, SUBLANES), SUBLANES)
            m = jnp.concatenate([mix_ref[t, rows, :] for t in range(n_tiles)], axis=1)
            x1 = x_ref[rows, :] + post * (m * _rms_scale(m))
            x1_ref[rows, :] = x1
            h2_ref[rows, :] = ((x1 * _rms_scale(x1)) * pre + sh).astype(BF16)
            return carry

        lax.fori_loop(0, bm // SUBLANES, rows_out, 0, unroll=8)


def _out_proj(a1, a2, w, x, gpost, gt, gpre, sc, sh, bm, tn):
    bsz, s, d = x.shape
    k1, k2 = a1.shape[2], a2.shape[2]
    n_tiles = d // tn
    vec = lambda: pl.BlockSpec((1, d), lambda b, i, j: (0, 0))
    bvec = lambda: pl.BlockSpec((None, 1, d), lambda b, i, j: (b, 0, 0))
    big = lambda: pl.BlockSpec((None, bm, d), lambda b, i, j: (b, i, 0), pipeline_mode=pl.Buffered(1))
    return pl.pallas_call(
        _outproj_kernel,
        out_shape=(jax.ShapeDtypeStruct((bsz, s, d), F32), jax.ShapeDtypeStruct((bsz, s, d), BF16)),
        grid=(bsz, s // bm, n_tiles),
        in_specs=[pl.BlockSpec((None, bm, k1), lambda b, i, j: (b, i, 0)),
                  pl.BlockSpec((None, bm, k2), lambda b, i, j: (b, i, 0)),
                  pl.BlockSpec((k1 + k2, tn), lambda b, i, j: (0, j)),
                  big(), vec(), bvec(), vec(), bvec(), bvec()],
        out_specs=(big(), big()),
        scratch_shapes=[pltpu.VMEM((n_tiles, bm, tn), F32)],
        compiler_params=_cparams(("arbitrary", "arbitrary", "arbitrary")),
        name="out_proj",
    )(a1, a2, w, x, gpost, gt, gpre, sc, sh)


FFN_ROW_CHUNK = 64


def _ffn_kernel(h_ref, wg_ref, wu_ref, cwg_ref, cwu_ref, cbg_ref, cbu_ref, wo_ref, x_ref, g_ref, gt_ref,
                o_ref, acc_ref, u_even, u_odd, act_even, act_odd, carry_ref):
    i = pl.program_id(1)
    s = pl.program_id(2)
    nj = pl.num_programs(2) - 2
    bm, d = h_ref.shape
    tf = wo_ref.shape[0]
    tile = jnp.minimum(s, nj - 1)

    @pl.when(s == 0)
    def _():
        acc_ref[...] = jnp.zeros_like(acc_ref)
        act_even[...] = jnp.zeros_like(act_even)
        u_odd[...] = jnp.zeros_like(u_odd)

    @pl.when(i == 0)
    def _():
        carry_ref[tile] = jnp.zeros(carry_ref.shape[1:], F32)

    def step(u_new, u_last, act_new, act_last):
        live = (s > 0).astype(F32)
        cwg, cwu, cbg, cbu = cwg_ref[...], cwu_ref[...], cbg_ref[...], cbu_ref[...]

        def conv(r0, c0, cw, cb):
            cols = slice(c0, c0 + tf)
            return (cb + cw[2:3, :] * u_last[r0:r0 + FFN_ROW_CHUNK, cols]
                    + cw[1:2, :] * u_last[r0 - 1:r0 - 1 + FFN_ROW_CHUNK, cols]
                    + cw[0:1, :] * u_last[r0 - 2:r0 - 2 + FFN_ROW_CHUNK, cols])

        def finish_rows(c, after):
            bits = lax.bitcast_convert_type(after[0:1, 0:tf], jnp.uint32)
            zero = lax.shift_right_logical(lax.shift_right_logical(bits, jnp.uint32(16)), jnp.uint32(16)).astype(F32)
            r0 = SUBLANES + c * FFN_ROW_CHUNK
            gate = conv(r0, 0, cwg, cbg + zero)
            up = conv(r0, tf, cwu, cbu + zero)
            act_new[c * FFN_ROW_CHUNK:(c + 1) * FFN_ROW_CHUNK, :] = (_gelu_tanh(gate) * up * live).astype(BF16)

        n_chunks = bm // FFN_ROW_CHUNK
        nw = d // n_chunks
        act_l = act_last[...]
        for c in range(n_chunks):
            cols = slice(c * nw, (c + 1) * nw)
            piece = jnp.dot(act_l, wo_ref[:, cols], preferred_element_type=F32)
            acc_ref[:, cols] += piece
            finish_rows(c, piece)

        w_gu = jnp.concatenate([wg_ref[...], wu_ref[...]], axis=1)
        u = jnp.dot(h_ref[...], w_gu, preferred_element_type=F32)
        u_new[0:SUBLANES, :] = carry_ref[tile]
        u_new[SUBLANES:, :] = u
        carry_ref[tile] = u[bm - SUBLANES:, :]

    parity = lax.rem(s, 2)

    @pl.when(parity == 0)
    def _():
        step(u_even, u_odd, act_odd, act_even)

    @pl.when(parity == 1)
    def _():
        step(u_odd, u_even, act_even, act_odd)

    @pl.when(s == nj + 1)
    def _():
        scale = gt_ref[...] * g_ref[...]

        def rows_out(r, carry):
            rows = pl.ds(pl.multiple_of(r * SUBLANES, SUBLANES), SUBLANES)
            f = acc_ref[rows, :]
            o_ref[rows, :] = x_ref[rows, :] + scale * (f * _rms_scale(f))
            return carry

        lax.fori_loop(0, bm // SUBLANES, rows_out, 0, unroll=8)


def _conv_ffn(h, w_in, cw, cb, wo, x, g, gt, bm, tf):
    bsz, s, d = x.shape
    nj = wo.shape[0] // tf
    kw = cw.shape[0]
    up_t = lambda j: jnp.minimum(j, nj - 1)
    ew_t = lambda j: jnp.clip(j - 1, 0, nj - 1)
    dn_t = lambda j: jnp.clip(j - 2, 0, nj - 1)
    big = lambda: pl.BlockSpec((None, bm, d), lambda b, i, j: (b, i, 0), pipeline_mode=pl.Buffered(1))
    return pl.pallas_call(
        _ffn_kernel,
        out_shape=jax.ShapeDtypeStruct((bsz, s, d), F32),
        grid=(bsz, s // bm, nj + 2),
        in_specs=[big(),
                  pl.BlockSpec((d, tf), lambda b, i, j: (0, up_t(j))),
                  pl.BlockSpec((d, tf), lambda b, i, j: (0, nj + up_t(j))),
                  pl.BlockSpec((kw, tf), lambda b, i, j: (0, ew_t(j))),
                  pl.BlockSpec((kw, tf), lambda b, i, j: (0, nj + ew_t(j))),
                  pl.BlockSpec((1, tf), lambda b, i, j: (0, ew_t(j))),
                  pl.BlockSpec((1, tf), lambda b, i, j: (0, nj + ew_t(j))),
                  pl.BlockSpec((tf, d), lambda b, i, j: (dn_t(j), 0)),
                  big(),
                  pl.BlockSpec((1, d), lambda b, i, j: (0, 0)),
                  pl.BlockSpec((None, 1, d), lambda b, i, j: (b, 0, 0))],
        out_specs=big(),
        scratch_shapes=[pltpu.VMEM((bm, d), F32),
                        pltpu.VMEM((SUBLANES + bm, 2 * tf), F32),
                        pltpu.VMEM((SUBLANES + bm, 2 * tf), F32),
                        pltpu.VMEM((bm, tf), BF16),
                        pltpu.VMEM((bm, tf), BF16),
                        pltpu.VMEM((nj, SUBLANES, 2 * tf), F32)],
        compiler_params=_cparams(("arbitrary", "arbitrary", "arbitrary")),
        name="conv_ffn",
    )(h, w_in, w_in, cw, cw, cb, cb, wo, x, g, gt)


def _layer(x, c, w_ada, b_ada, g_pre_mix, g_post_mix, g_pre_ffn, g_post_ffn, w_in,
           conv_lru_w, conv_lru_b, w_rg_a, b_rg_a, w_rg_x, b_rg_x, lru_lambda,
           g_grp_lru, g_grp_attn, w_out, w_ffn_in, conv_ffn_w, conv_ffn_b, w_ffn_out,
           *, bm_in, bn_in, bt_lru, bq, kb_sz, bm_out, tn_out, bm_ffn, tf):
    bsz, s, d = x.shape
    d_lru = w_rg_a.shape[0] * LRU_BLOCK
    d_attn = g_grp_attn.shape[0]
    d_qi = N_IDX_HEADS * IDX_DIM
    d_kv = (w_in.shape[1] - 2 * d_lru - d_attn - d_qi - IDX_DIM - N_IDX_HEADS) // 2
    row = lambda v: v.reshape(1, -1)

    mod = _modulation(c, w_ada, b_ada)
    sh1, sc1, gt1, sh2, sc2, gt2 = [m.reshape(bsz, 1, d) for m in jnp.split(mod, 6, axis=-1)]

    q_off = 2 * d_lru
    k_off = q_off + d_attn
    qi_off = k_off + 2 * d_kv
    n_main = qi_off + d_qi
    n_small = w_in.shape[1] - n_main
    w_in_b = w_in.astype(BF16)
    w_small = jnp.concatenate([w_in[:, n_main:], jnp.zeros((d, LANES - n_small), F32)], axis=1).astype(BF16)

    proj, small = _in_proj(x, row(g_pre_mix), sc1, sh1, w_in_b, n_main, w_small, bm_in, bn_in)

    y_lru = _rglru(proj, 0, conv_lru_w, row(conv_lru_b), w_rg_a.astype(BF16), row(b_rg_a),
                   w_rg_x.astype(BF16), row(b_rg_x), row(lru_lambda), row(g_grp_lru), bt_lru)

    ki = small[:, :, :IDX_DIM].astype(BF16).reshape(bsz, s // kb_sz, kb_sz, IDX_DIM)
    zz = jnp.zeros_like(ki)
    kd = jnp.concatenate([jnp.concatenate([ki, zz], axis=3), jnp.concatenate([zz, ki], axis=3)], axis=2)
    y_attn = _dsa(proj, q_off, qi_off, k_off, small, kd, row(g_grp_attn), d_attn, d_kv, bq)

    x1, h2 = _out_proj(y_lru, y_attn, w_out.astype(BF16), x, row(g_post_mix), gt1, row(g_pre_ffn), sc2, sh2,
                       bm_out, tn_out)

    return _conv_ffn(h2, w_ffn_in.astype(BF16), conv_ffn_w, row(conv_ffn_b), w_ffn_out.astype(BF16),
                     x1, row(g_post_ffn), gt2, bm_ffn, tf)


_TILES = dict(bm_in=512, bn_in=1024, bt_lru=256, bq=256, kb_sz=512, bm_out=512, tn_out=512, bm_ffn=512, tf=256)


def kernel(x, c, w_ada, b_ada, g_pre_mix, g_post_mix, g_pre_ffn, g_post_ffn, w_in, conv_lru_w, conv_lru_b,
           w_rg_a, b_rg_a, w_rg_x, b_rg_x, lru_lambda, g_grp_lru, g_grp_attn, w_out, w_ffn_in,
           conv_ffn_w, conv_ffn_b, w_ffn_out):
    for l in range(w_ada.shape[0]):
        x = _layer(x, c, w_ada[l], b_ada[l], g_pre_mix[l], g_post_mix[l], g_pre_ffn[l], g_post_ffn[l],
                   w_in[l], conv_lru_w[l], conv_lru_b[l], w_rg_a[l], b_rg_a[l], w_rg_x[l], b_rg_x[l],
                   lru_lambda[l], g_grp_lru[l], g_grp_attn[l], w_out[l], w_ffn_in[l], conv_ffn_w[l],
                   conv_ffn_b[l], w_ffn_out[l], **_TILES)
    return x
```

```python
import functools
import math

import jax
import jax.numpy as jnp
from jax import lax
from jax.experimental import pallas as pl
from jax.experimental.pallas import tpu as pltpu

F32 = jnp.float32
BF16 = jnp.bfloat16
I32 = jnp.int32

RMS_EPS = 1e-6
LRU_C = 8.0
LRU_BLOCK = 128
HEAD_DIM = 128
N_KV_HEADS = 4
N_IDX_HEADS = 32
IDX_DIM = 64
TOPK_MAX = 256

LANES = 128
SUBLANES = 8
VMEM_LIMIT = 56 * 1024 * 1024

ATTN_HEADS_PER_SWEEP = 2
COUNT_CHAINS = 2
INT_MIN = -(2 ** 31)
NEG_BIG = -1e30
LOG2_E = 1.4426950408889634


def _cparams(sem, flags=None):
    return pltpu.CompilerParams(dimension_semantics=sem, vmem_limit_bytes=VMEM_LIMIT, flags=flags)


def _rms_scale(v):
    return lax.rsqrt(jnp.mean(v * v, axis=-1, keepdims=True) + RMS_EPS)


def _gelu_tanh(v):
    inner = v * (1.0 + 0.044715 * (v * v))
    return v / (1.0 + jnp.exp((-2.0 * 0.7978845608028654) * inner))


def _shift_rows(cur, prev8, d):
    rolled = pltpu.roll(cur, d, 0)
    head = pltpu.roll(prev8, d, 0)
    row = lax.broadcasted_iota(I32, (SUBLANES, cur.shape[1]), 0)
    top = jnp.where(row < d, head, rolled[0:SUBLANES])
    return jnp.concatenate([top, rolled[SUBLANES:]], axis=0)


def _mod_kernel(c_ref, w_ref, b_ref, o_ref):
    c = c_ref[...]
    s = c * jax.nn.sigmoid(c)
    o_ref[...] = jnp.dot(s, w_ref[...], preferred_element_type=F32,
                         precision=lax.Precision.HIGHEST) + b_ref[...]


def _modulation(c, w_ada, b_ada, tn=512):
    bsz, d = c.shape
    n = w_ada.shape[1]
    rows = -(-bsz // SUBLANES) * SUBLANES
    c_pad = jnp.zeros((rows, d), F32).at[:bsz].set(c)
    out = pl.pallas_call(
        _mod_kernel,
        out_shape=jax.ShapeDtypeStruct((rows, n), F32),
        grid=(n // tn,),
        in_specs=[pl.BlockSpec((rows, d), lambda j: (0, 0)),
                  pl.BlockSpec((d, tn), lambda j: (0, j)),
                  pl.BlockSpec((1, tn), lambda j: (0, j))],
        out_specs=pl.BlockSpec((rows, tn), lambda j: (0, j)),
        compiler_params=_cparams(("arbitrary",)),
        name="modulation",
    )(c_pad, w_ada, b_ada.reshape(1, n))
    return out[:bsz]


def _inproj_kernel(x_ref, g_ref, sc_ref, sh_ref, w_ref, ws_ref, o_ref, os_ref, h_ref):
    @pl.when(pl.program_id(2) == 0)
    def _():
        x = x_ref[...]
        y = x * _rms_scale(x) * g_ref[...]
        h = (y * (1.0 + sc_ref[...]) + sh_ref[...]).astype(BF16)
        h_ref[...] = h
        os_ref[...] = jnp.dot(h, ws_ref[...], preferred_element_type=F32)

    o_ref[...] = jnp.dot(h_ref[...], w_ref[...], preferred_element_type=F32).astype(BF16)


def _in_proj(x, g, sc, sh, w_main, n, w_small, bm, bn):
    bsz, s, d = x.shape
    ns = w_small.shape[1]
    return pl.pallas_call(
        _inproj_kernel,
        out_shape=(jax.ShapeDtypeStruct((bsz, s, n), BF16), jax.ShapeDtypeStruct((bsz, s, ns), F32)),
        grid=(bsz, s // bm, n // bn),
        in_specs=[pl.BlockSpec((None, bm, d), lambda b, i, j: (b, i, 0)),
                  pl.BlockSpec((1, d), lambda b, i, j: (0, 0)),
                  pl.BlockSpec((None, 1, d), lambda b, i, j: (b, 0, 0)),
                  pl.BlockSpec((None, 1, d), lambda b, i, j: (b, 0, 0)),
                  pl.BlockSpec((d, bn), lambda b, i, j: (0, j)),
                  pl.BlockSpec((d, ns), lambda b, i, j: (0, 0))],
        out_specs=(pl.BlockSpec((None, bm, bn), lambda b, i, j: (b, i, j)),
                   pl.BlockSpec((None, bm, ns), lambda b, i, j: (b, i, 0))),
        scratch_shapes=[pltpu.VMEM((bm, d), BF16)],
        compiler_params=_cparams(("arbitrary", "arbitrary", "arbitrary")),
        name="in_proj",
    )(x, g, sc, sh, w_main, w_small)


def _rglru_kernel(xl_ref, gate_ref, cw_ref, cb_ref, wa_ref, ba_ref, wx_ref, bx_ref, lam_ref, g_ref,
                  o_ref, halo_ref, hc_ref, a_scr, b_scr):
    bt, dl = xl_ref.shape
    nblk = dl // LRU_BLOCK

    @pl.when(pl.program_id(1) == 0)
    def _():
        halo_ref[...] = jnp.zeros_like(halo_ref)
        hc_ref[...] = jnp.zeros_like(hc_ref)

    xl = xl_ref[...].astype(F32)
    prev = halo_ref[...]
    halo_ref[...] = xl[bt - SUBLANES:, :]

    cw = cw_ref[...]
    kw = cw.shape[0]
    xc = cb_ref[...] + cw[kw - 1:kw, :] * xl
    for dlt in range(1, kw):
        xc = xc + cw[kw - 1 - dlt:kw - dlt, :] * _shift_rows(xl, prev, dlt)

    xcb = xc.astype(BF16)
    r_parts, i_parts = [], []
    for n in range(nblk):
        blk = xcb[:, n * LRU_BLOCK:(n + 1) * LRU_BLOCK]
        r_parts.append(jnp.dot(blk, wa_ref[n], preferred_element_type=F32))
        i_parts.append(jnp.dot(blk, wx_ref[n], preferred_element_type=F32))
    r = jax.nn.sigmoid(jnp.concatenate(r_parts, axis=1) + ba_ref[...])
    ig = jax.nn.sigmoid(jnp.concatenate(i_parts, axis=1) + bx_ref[...])

    z = -lam_ref[...]
    softplus = jnp.maximum(z, 0.0) + jnp.log(1.0 + jnp.exp(-jnp.abs(z)))
    log_a = (-LRU_C * softplus) * r
    a = jnp.exp(log_a)
    u = jnp.sqrt(1.0 - jnp.exp(2.0 * log_a)) * (ig * xc)

    rowmod = lax.broadcasted_iota(I32, (bt, dl), 0) % SUBLANES
    b = u
    for sft in (1, 2, 4):
        keep = rowmod >= sft
        a_prev = jnp.where(keep, pltpu.roll(a, sft, 0), 1.0)
        b_prev = jnp.where(keep, pltpu.roll(b, sft, 0), 0.0)
        b = a * b_prev + b
        a = a * a_prev
    a_scr[...] = a
    b_scr[...] = b

    def group(gi, carry):
        rows = pl.ds(pl.multiple_of(gi * SUBLANES, SUBLANES), SUBLANES)
        h = a_scr[rows, :] * carry + b_scr[rows, :]
        b_scr[rows, :] = h
        return jnp.broadcast_to(h[SUBLANES - 1:SUBLANES, :], (SUBLANES, dl))

    hc_ref[...] = lax.fori_loop(0, bt // SUBLANES, group, hc_ref[...])

    y = b_scr[...] * _gelu_tanh(gate_ref[...].astype(F32))
    o_ref[...] = (y * _rms_scale(y) * g_ref[...]).astype(BF16)


def _rglru(proj, x_off, cw, cb, wa, ba, wx, bx, lam, g, bt):
    bsz, s, _ = proj.shape
    dl = cw.shape[1]
    nblk = dl // LRU_BLOCK
    x_blk = x_off // dl
    vec = lambda: pl.BlockSpec((1, dl), lambda b, t: (0, 0))
    return pl.pallas_call(
        _rglru_kernel,
        out_shape=jax.ShapeDtypeStruct((bsz, s, dl), BF16),
        grid=(bsz, s // bt),
        in_specs=[pl.BlockSpec((None, bt, dl), lambda b, t: (b, t, x_blk)),
                  pl.BlockSpec((None, bt, dl), lambda b, t: (b, t, x_blk + 1)),
                  pl.BlockSpec((cw.shape[0], dl), lambda b, t: (0, 0)),
                  vec(),
                  pl.BlockSpec((nblk, LRU_BLOCK, LRU_BLOCK), lambda b, t: (0, 0, 0)),
                  vec(),
                  pl.BlockSpec((nblk, LRU_BLOCK, LRU_BLOCK), lambda b, t: (0, 0, 0)),
                  vec(), vec(), vec()],
        out_specs=pl.BlockSpec((None, bt, dl), lambda b, t: (b, t, 0)),
        scratch_shapes=[pltpu.VMEM((SUBLANES, dl), F32), pltpu.VMEM((SUBLANES, dl), F32),
                        pltpu.VMEM((bt, dl), F32), pltpu.VMEM((bt, dl), F32)],
        compiler_params=_cparams(("arbitrary", "arbitrary")),
        name="rglru",
    )(proj, proj, cw, cb, wa, ba, wx, bx, lam, g)


def _dsa_kernel(q_ref, *refs, k_sel, rep, n_qi):
    qi_refs = refs[:n_qi]
    (k_ref, v_ref, ki_ref, w_ref, g_ref, o_ref,
     key_scr, bias_scr, m_scr, l_scr, acc_scr, out_scr) = refs[n_qi:]
    bq = q_ref.shape[0]
    nkb_all, kb_sz, _ = key_scr.shape
    n_groups = q_ref.shape[1] // (rep * HEAD_DIM)
    qblk = pl.program_id(1)
    nkb = lax.div((qblk + 1) * bq + kb_sz - 1, kb_sz)
    qi_w = qi_refs[0].shape[1]
    n_idx_heads = n_qi * qi_w // IDX_DIM
    lane_tiles = kb_sz // LANES
    tile = lambda v: jnp.concatenate([v] * lane_tiles, axis=1)
    down = lambda v: jnp.concatenate([v] * (kb_sz // SUBLANES), axis=0)

    w_t = jnp.transpose(w_ref[...])
    w_off = IDX_DIM
    q_heads = []
    for h in range(n_idx_heads):
        piece, col = divmod(h * IDX_DIM, qi_w)
        q_heads.append(qi_refs[piece][:, col:col + IDX_DIM])

    def score_block(kb, carry):
        rows = pl.ds(pl.multiple_of(kb * kb_sz, kb_sz), kb_sz)
        ki = ki_ref[rows, 0:IDX_DIM].astype(BF16)
        acc = jnp.zeros((kb_sz, bq), F32)
        for h, qh in enumerate(q_heads):
            dots = lax.dot_general(ki, qh, (((1,), (1,)), ((), ())), preferred_element_type=F32)
            acc = acc + jnp.maximum(dots, 0.0) * w_t[w_off + h:w_off + h + 1, :]
        bits = lax.bitcast_convert_type(acc, I32)
        key = jnp.where(bits < 0, bits ^ jnp.int32(0x7FFFFFFF), bits)
        kpos = kb * kb_sz + lax.broadcasted_iota(I32, (kb_sz, bq), 0)
        qpos = qblk * bq + lax.broadcasted_iota(I32, (kb_sz, bq), 1)
        key_scr[kb] = jnp.where(kpos <= qpos, key, jnp.int32(INT_MIN))
        return carry

    lax.fori_loop(0, nkb, score_block, 0)

    def count_keys(hit_fn):
        lanes = COUNT_CHAINS

        def body(kb, c):
            ones = jnp.where(hit_fn(kb), 1.0, 0.0)
            return c + jnp.sum(ones.reshape(lanes, kb_sz // (lanes * SUBLANES), SUBLANES, bq), axis=1)
        c = lax.fori_loop(0, nkb, body, jnp.zeros((lanes, SUBLANES, bq), F32))
        return jnp.broadcast_to(jnp.sum(c.reshape(lanes * SUBLANES, bq), axis=0, keepdims=True), (SUBLANES, bq))

    def count_ge(cand):
        cand_rows = down(cand)
        return count_keys(lambda kb: key_scr[kb] >= cand_rows)

    def bit_step(it, carry):
        ans, n_ge = carry
        cand_u = ans | lax.shift_left(jnp.int32(1), 31 - it)
        cnt = count_ge(cand_u ^ jnp.int32(INT_MIN))
        ok = cnt >= k_sel
        return jnp.where(ok, cand_u, ans), jnp.where(ok, cnt, n_ge)

    ans, n_ge = lax.fori_loop(0, 32, bit_step,
                              (jnp.zeros((SUBLANES, bq), I32), jnp.zeros((SUBLANES, bq), F32)))
    thr = jnp.maximum(ans ^ jnp.int32(INT_MIN), jnp.int32(INT_MIN + 1))
    thr_rows = down(thr)

    @pl.when(jnp.max(n_ge) > k_sel)
    def _():
        need = k_sel - count_ge(thr + 1)
        pos_bits = (nkb_all * kb_sz - 1).bit_length()

        def ties_before(pos):
            pos_rows = down(pos)

            def hit(kb):
                kpos = kb * kb_sz + lax.broadcasted_iota(I32, (kb_sz, bq), 0)
                return (key_scr[kb] == thr_rows) & (kpos < pos_rows)
            return count_keys(hit)

        def pos_step(it, last):
            cand = last | lax.shift_left(jnp.int32(1), pos_bits - 1 - it)
            return jnp.where(ties_before(cand) < need, cand, last)

        last_rows = down(lax.fori_loop(0, pos_bits, pos_step, jnp.zeros((SUBLANES, bq), I32)))

        def demote(kb, carry):
            key = key_scr[kb]
            kpos = kb * kb_sz + lax.broadcasted_iota(I32, (kb_sz, bq), 0)
            key_scr[kb] = jnp.where((key == thr_rows) & (kpos > last_rows), key - 1, key)
            return carry

        lax.fori_loop(0, nkb, demote, 0)

    def to_bias(kb, carry):
        bias = jnp.where(key_scr[kb] >= thr_rows, 0.0, NEG_BIG).astype(F32)
        bias_scr[kb] = jnp.transpose(bias)
        return carry

    lax.fori_loop(0, nkb, to_bias, 0)

    c_exp = (HEAD_DIM ** -0.5) * LOG2_E
    sweep = m_scr.shape[0]
    for g0 in range(0, n_groups, sweep):
        heads = range(g0, g0 + sweep)
        qss = [jnp.concatenate(
            [q_ref[:, (g * rep + r) * HEAD_DIM:(g * rep + r + 1) * HEAD_DIM] for r in range(rep)], axis=0)
            for g in heads]
        m_scr[...] = jnp.full_like(m_scr, NEG_BIG)
        l_scr[...] = jnp.zeros_like(l_scr)
        acc_scr[...] = jnp.zeros_like(acc_scr)

        def kv_block(kb, carry):
            rows = pl.ds(pl.multiple_of(kb * kb_sz, kb_sz), kb_sz)
            bias = jnp.concatenate([bias_scr[kb]] * rep, axis=0)
            logits = [lax.dot_general(qs, k_ref[rows, g * HEAD_DIM:(g + 1) * HEAD_DIM],
                                      (((1,), (1,)), ((), ())), preferred_element_type=F32)
                      for qs, g in zip(qss, heads)]
            for slot, (s, g) in enumerate(zip(logits, heads)):
                s = s + bias
                m_prev = m_scr[slot]
                m_new = jnp.maximum(m_prev, jnp.max(s, axis=1, keepdims=True))
                alpha = jnp.exp2((m_prev - m_new) * c_exp)
                p = jnp.exp2((s - tile(m_new)) * c_exp)
                l_scr[slot] = alpha * l_scr[slot] + jnp.sum(p, axis=1, keepdims=True)
                acc_scr[slot] = alpha * acc_scr[slot] + jnp.dot(
                    p.astype(BF16), v_ref[rows, g * HEAD_DIM:(g + 1) * HEAD_DIM], preferred_element_type=F32)
                m_scr[slot] = m_new
            return carry

        lax.fori_loop(0, nkb, kv_block, 0)
        for slot, g in enumerate(heads):
            o = acc_scr[slot] / l_scr[slot]
            for r in range(rep):
                out_scr[:, (g * rep + r) * HEAD_DIM:(g * rep + r + 1) * HEAD_DIM] = o[r * bq:(r + 1) * bq, :]

    y = out_scr[...]
    o_ref[...] = (y * _rms_scale(y) * g_ref[...]).astype(BF16)


def _dsa(proj, q_off, qi_off, k_off, small, g, d_attn, d_kv, bq, kb_sz):
    bsz, s, _ = proj.shape
    nkb_all = s // kb_sz
    rep = d_attn // d_kv
    k_sel = min(TOPK_MAX, s // 4)
    d_qi = N_IDX_HEADS * IDX_DIM
    qi_w = math.gcd(qi_off, d_qi)
    n_qi = d_qi // qi_w
    q_blk = q_off // d_attn
    k_blk = k_off // d_kv
    kern = functools.partial(_dsa_kernel, k_sel=k_sel, rep=rep, n_qi=n_qi)
    qi_specs = [pl.BlockSpec((None, bq, qi_w), lambda b, i, t=t: (b, i, qi_off // qi_w + t)) for t in range(n_qi)]
    return pl.pallas_call(
        kern,
        out_shape=jax.ShapeDtypeStruct((bsz, s, d_attn), BF16),
        grid=(bsz, s // bq),
        in_specs=[pl.BlockSpec((None, bq, d_attn), lambda b, i: (b, i, q_blk))] + qi_specs + [
                  pl.BlockSpec((None, s, d_kv), lambda b, i: (b, 0, k_blk)),
                  pl.BlockSpec((None, s, d_kv), lambda b, i: (b, 0, k_blk + 1)),
                  pl.BlockSpec((None, s, LANES), lambda b, i: (b, 0, 0)),
                  pl.BlockSpec((None, bq, LANES), lambda b, i: (b, i, 0)),
                  pl.BlockSpec((1, d_attn), lambda b, i: (0, 0))],
        out_specs=pl.BlockSpec((None, bq, d_attn), lambda b, i: (b, i, 0)),
        scratch_shapes=[pltpu.VMEM((nkb_all, kb_sz, bq), I32),
                        pltpu.VMEM((nkb_all, bq, kb_sz), F32),
                        pltpu.VMEM((ATTN_HEADS_PER_SWEEP, rep * bq, LANES), F32),
                        pltpu.VMEM((ATTN_HEADS_PER_SWEEP, rep * bq, LANES), F32),
                        pltpu.VMEM((ATTN_HEADS_PER_SWEEP, rep * bq, HEAD_DIM), F32),
                        pltpu.VMEM((bq, d_attn), F32)],
        compiler_params=_cparams(("arbitrary", "arbitrary")),
        name="dsa",
    )(proj, *([proj] * n_qi), proj, proj, small, small, g)


def _outproj_kernel(a1_ref, a2_ref, w_ref, x_ref, gpost_ref, gt_ref, gpre_ref, sc_ref, sh_ref,
                    x1_ref, h2_ref, mix_ref):
    j = pl.program_id(2)
    n_tiles, bm, tn = mix_ref.shape
    k1 = a1_ref.shape[1]
    mix_ref[j] = (jnp.dot(a1_ref[...], w_ref[0:k1, :], preferred_element_type=F32)
                  + jnp.dot(a2_ref[...], w_ref[k1:, :], preferred_element_type=F32))

    @pl.when(j == n_tiles - 1)
    def _():
        post = gt_ref[...] * gpost_ref[...]
        pre = gpre_ref[...] * (1.0 + sc_ref[...])
        sh = sh_ref[...]

        def rows_out(r, carry):
            rows = pl.ds(pl.multiple_of(r * SUBLANES, SUBLANES), SUBLANES)
            m = jnp.concatenate([mix_ref[t, rows, :] for t in range(n_tiles)], axis=1)
            x1 = x_ref[rows, :] + post * (m * _rms_scale(m))
            x1_ref[rows, :] = x1
            h2_ref[rows, :] = ((x1 * _rms_scale(x1)) * pre + sh).astype(BF16)
            return carry

        lax.fori_loop(0, bm // SUBLANES, rows_out, 0, unroll=8)


def _out_proj(a1, a2, w, x, gpost, gt, gpre, sc, sh, bm, tn):
    bsz, s, d = x.shape
    k1, k2 = a1.shape[2], a2.shape[2]
    n_tiles = d // tn
    vec = lambda: pl.BlockSpec((1, d), lambda b, i, j: (0, 0))
    bvec = lambda: pl.BlockSpec((None, 1, d), lambda b, i, j: (b, 0, 0))
    big = lambda: pl.BlockSpec((None, bm, d), lambda b, i, j: (b, i, 0), pipeline_mode=pl.Buffered(1))
    return pl.pallas_call(
        _outproj_kernel,
        out_shape=(jax.ShapeDtypeStruct((bsz, s, d), F32), jax.ShapeDtypeStruct((bsz, s, d), BF16)),
        grid=(bsz, s // bm, n_tiles),
        in_specs=[pl.BlockSpec((None, bm, k1), lambda b, i, j: (b, i, 0)),
                  pl.BlockSpec((None, bm, k2), lambda b, i, j: (b, i, 0)),
                  pl.BlockSpec((k1 + k2, tn), lambda b, i, j: (0, j)),
                  big(), vec(), bvec(), vec(), bvec(), bvec()],
        out_specs=(big(), big()),
        scratch_shapes=[pltpu.VMEM((n_tiles, bm, tn), F32)],
        compiler_params=_cparams(("arbitrary", "arbitrary", "arbitrary")),
        name="out_proj",
    )(a1, a2, w, x, gpost, gt, gpre, sc, sh)


FFN_ROW_CHUNK = 64


def _ffn_kernel(h_ref, wg_ref, wu_ref, cwg_ref, cwu_ref, cbg_ref, cbu_ref, wo_ref, x_ref, g_ref, gt_ref,
                o_ref, acc_ref, u_even, u_odd, act_even, act_odd, carry_ref):
    i = pl.program_id(1)
    s = pl.program_id(2)
    nj = pl.num_programs(2) - 2
    bm, d = h_ref.shape
    tf = wo_ref.shape[0]
    tile = jnp.minimum(s, nj - 1)
    u_bufs = (u_even, u_odd)
    act_bufs = (act_even, act_odd)

    @pl.when(s == 0)
    def _():
        acc_ref[...] = jnp.zeros_like(acc_ref)

    @pl.when(jnp.logical_and(i == 0, s < nj))
    def _():
        carry_ref[tile] = jnp.zeros(carry_ref.shape[1:], F32)

    def step(p, up_proj, finish, down_proj):
        u_new, u_last = u_bufs[p], u_bufs[1 - p]
        act_new, act_last = act_bufs[1 - p], act_bufs[p]
        cwg, cwu, cbg, cbu = cwg_ref[...], cwu_ref[...], cbg_ref[...], cbu_ref[...]

        def conv(r0, c0, cw, cb):
            cols = slice(c0, c0 + tf)
            return (cb + cw[2:3, :] * u_last[r0:r0 + FFN_ROW_CHUNK, cols]
                    + cw[1:2, :] * u_last[r0 - 1:r0 - 1 + FFN_ROW_CHUNK, cols]
                    + cw[0:1, :] * u_last[r0 - 2:r0 - 2 + FFN_ROW_CHUNK, cols])

        def finish_rows(c, after):
            if after is None:
                zero = 0.0
            else:
                bits = lax.bitcast_convert_type(after[0:1, 0:tf], jnp.uint32)
                zero = lax.shift_right_logical(lax.shift_right_logical(bits, jnp.uint32(16)),
                                               jnp.uint32(16)).astype(F32)
            r0 = SUBLANES + c * FFN_ROW_CHUNK
            gate = conv(r0, 0, cwg, cbg + zero)
            up = conv(r0, tf, cwu, cbu + zero)
            act_new[c * FFN_ROW_CHUNK:(c + 1) * FFN_ROW_CHUNK, :] = (_gelu_tanh(gate) * up).astype(BF16)

        n_chunks = bm // FFN_ROW_CHUNK
        nw = d // n_chunks
        act_l = act_last[...] if down_proj else None
        for c in range(n_chunks):
            piece = None
            if down_proj:
                cols = slice(c * nw, (c + 1) * nw)
                piece = jnp.dot(act_l, wo_ref[:, cols], preferred_element_type=F32)
                acc_ref[:, cols] += piece
            if finish:
                finish_rows(c, piece)

        if up_proj:
            w_gu = jnp.concatenate([wg_ref[...], wu_ref[...]], axis=1)
            u = jnp.dot(h_ref[...], w_gu, preferred_element_type=F32)
            u_new[0:SUBLANES, :] = carry_ref[tile]
            u_new[SUBLANES:, :] = u
            carry_ref[tile] = u[bm - SUBLANES:, :]

    parity = lax.rem(s, 2)
    steady = jnp.logical_and(s >= 2, s < nj)
    variants = [(s == 0, 0, (True, False, False)),
                (s == 1, 1, (True, True, False)),
                (jnp.logical_and(steady, parity == 0), 0, (True, True, True)),
                (jnp.logical_and(steady, parity == 1), 1, (True, True, True)),
                (s == nj, nj % 2, (False, True, True)),
                (s == nj + 1, (nj + 1) % 2, (False, False, True))]
    for cond, p, stages in variants:
        pl.when(cond)(functools.partial(step, p, *stages))

    @pl.when(s == nj + 1)
    def _():
        scale = gt_ref[...] * g_ref[...]

        def rows_out(r, carry):
            rows = pl.ds(pl.multiple_of(r * SUBLANES, SUBLANES), SUBLANES)
            f = acc_ref[rows, :]
            o_ref[rows, :] = x_ref[rows, :] + scale * (f * _rms_scale(f))
            return carry

        lax.fori_loop(0, bm // SUBLANES, rows_out, 0, unroll=8)


def _conv_ffn(h, w_in, cw, cb, wo, x, g, gt, bm, tf):
    bsz, s, d = x.shape
    nj = wo.shape[0] // tf
    kw = cw.shape[0]
    up_t = lambda j: jnp.minimum(j, nj - 1)
    ew_t = lambda j: jnp.clip(j - 1, 0, nj - 1)
    dn_t = lambda j: jnp.clip(j - 2, 0, nj - 1)
    big = lambda: pl.BlockSpec((None, bm, d), lambda b, i, j: (b, i, 0), pipeline_mode=pl.Buffered(1))
    return pl.pallas_call(
        _ffn_kernel,
        out_shape=jax.ShapeDtypeStruct((bsz, s, d), F32),
        grid=(bsz, s // bm, nj + 2),
        in_specs=[big(),
                  pl.BlockSpec((d, tf), lambda b, i, j: (0, up_t(j))),
                  pl.BlockSpec((d, tf), lambda b, i, j: (0, nj + up_t(j))),
                  pl.BlockSpec((kw, tf), lambda b, i, j: (0, ew_t(j))),
                  pl.BlockSpec((kw, tf), lambda b, i, j: (0, nj + ew_t(j))),
                  pl.BlockSpec((1, tf), lambda b, i, j: (0, ew_t(j))),
                  pl.BlockSpec((1, tf), lambda b, i, j: (0, nj + ew_t(j))),
                  pl.BlockSpec((tf, d), lambda b, i, j: (dn_t(j), 0)),
                  big(),
                  pl.BlockSpec((1, d), lambda b, i, j: (0, 0)),
                  pl.BlockSpec((None, 1, d), lambda b, i, j: (b, 0, 0))],
        out_specs=big(),
        scratch_shapes=[pltpu.VMEM((bm, d), F32),
                        pltpu.VMEM((SUBLANES + bm, 2 * tf), F32),
                        pltpu.VMEM((SUBLANES + bm, 2 * tf), F32),
                        pltpu.VMEM((bm, tf), BF16),
                        pltpu.VMEM((bm, tf), BF16),
                        pltpu.VMEM((nj, SUBLANES, 2 * tf), F32)],
        compiler_params=_cparams(("arbitrary", "arbitrary", "arbitrary")),
        name="conv_ffn",
    )(h, w_in, w_in, cw, cw, cb, cb, wo, x, g, gt)


def _layer(x, c, w_ada, b_ada, g_pre_mix, g_post_mix, g_pre_ffn, g_post_ffn, w_in,
           conv_lru_w, conv_lru_b, w_rg_a, b_rg_a, w_rg_x, b_rg_x, lru_lambda,
           g_grp_lru, g_grp_attn, w_out, w_ffn_in, conv_ffn_w, conv_ffn_b, w_ffn_out,
           *, bm_in, bn_in, bt_lru, bq, kb_sz, bm_out, tn_out, bm_ffn, tf):
    bsz, s, d = x.shape
    d_lru = w_rg_a.shape[0] * LRU_BLOCK
    d_attn = g_grp_attn.shape[0]
    d_qi = N_IDX_HEADS * IDX_DIM
    d_kv = (w_in.shape[1] - 2 * d_lru - d_attn - d_qi - IDX_DIM - N_IDX_HEADS) // 2
    row = lambda v: v.reshape(1, -1)

    mod = _modulation(c, w_ada, b_ada)
    sh1, sc1, gt1, sh2, sc2, gt2 = [m.reshape(bsz, 1, d) for m in jnp.split(mod, 6, axis=-1)]

    q_off = 2 * d_lru
    k_off = q_off + d_attn
    qi_off = k_off + 2 * d_kv
    n_main = qi_off + d_qi
    n_small = w_in.shape[1] - n_main
    w_in_b = w_in.astype(BF16)
    w_small = jnp.concatenate([w_in[:, n_main:], jnp.zeros((d, LANES - n_small), F32)], axis=1).astype(BF16)

    proj, small = _in_proj(x, row(g_pre_mix), sc1, sh1, w_in_b, n_main, w_small, bm_in, bn_in)

    y_lru = _rglru(proj, 0, conv_lru_w, row(conv_lru_b), w_rg_a.astype(BF16), row(b_rg_a),
                   w_rg_x.astype(BF16), row(b_rg_x), row(lru_lambda), row(g_grp_lru), bt_lru)

    y_attn = _dsa(proj, q_off, qi_off, k_off, small, row(g_grp_attn), d_attn, d_kv, bq, kb_sz)

    x1, h2 = _out_proj(y_lru, y_attn, w_out.astype(BF16), x, row(g_post_mix), gt1, row(g_pre_ffn), sc2, sh2,
                       bm_out, tn_out)

    return _conv_ffn(h2, w_ffn_in.astype(BF16), conv_ffn_w, row(conv_ffn_b), w_ffn_out.astype(BF16),
                     x1, row(g_post_ffn), gt2, bm_ffn, tf)


_TILES = dict(bm_in=512, bn_in=1024, bt_lru=256, bq=256, kb_sz=512, bm_out=512, tn_out=512, bm_ffn=512, tf=256)


def kernel(x, c, w_ada, b_ada, g_pre_mix, g_post_mix, g_pre_ffn, g_post_ffn, w_in, conv_lru_w, conv_lru_b,
           w_rg_a, b_rg_a, w_rg_x, b_rg_x, lru_lambda, g_grp_lru, g_grp_attn, w_out, w_ffn_in,
           conv_ffn_w, conv_ffn_b, w_ffn_out):
    for l in range(w_ada.shape[0]):
        x = _layer(x, c, w_ada[l], b_ada[l], g_pre_mix[l], g_post_mix[l], g_pre_ffn[l], g_post_ffn[l],
                   w_in[l], conv_lru_w[l], conv_lru_b[l], w_rg_a[l], b_rg_a[l], w_rg_x[l], b_rg_x[l],
                   lru_lambda[l], g_grp_lru[l], g_grp_attn[l], w_out[l], w_ffn_in[l], conv_ffn_w[l],
                   conv_ffn_b[l], w_ffn_out[l], **_TILES)
    return x
```

```python
import functools
import math

import jax
import jax.numpy as jnp
from jax import lax
from jax.experimental import pallas as pl
from jax.experimental.pallas import tpu as pltpu

F32 = jnp.float32
BF16 = jnp.bfloat16
I32 = jnp.int32

RMS_EPS = 1e-6
LRU_C = 8.0
LRU_BLOCK = 128
HEAD_DIM = 128
N_KV_HEADS = 4
N_IDX_HEADS = 32
IDX_DIM = 64
TOPK_MAX = 256

LANES = 128
SUBLANES = 8
VMEM_LIMIT = 59 * 1024 * 1024

ATTN_HEADS_PER_SWEEP = 2
COUNT_CHAINS = 2
INT_MIN = -(2 ** 31)
NEG_BIG = -1e30
LOG2_E = 1.4426950408889634


def _cparams(sem):
    return pltpu.CompilerParams(dimension_semantics=sem, vmem_limit_bytes=VMEM_LIMIT)


def _rms_scale(v):
    return lax.rsqrt(jnp.mean(v * v, axis=-1, keepdims=True) + RMS_EPS)


def _gelu_tanh(v):
    inner = v * (1.0 + 0.044715 * (v * v))
    return v / (1.0 + jnp.exp((-2.0 * 0.7978845608028654) * inner))


def _shift_rows(cur, prev8, d):
    rolled = pltpu.roll(cur, d, 0)
    head = pltpu.roll(prev8, d, 0)
    row = lax.broadcasted_iota(I32, (SUBLANES, cur.shape[1]), 0)
    top = jnp.where(row < d, head, rolled[0:SUBLANES])
    return jnp.concatenate([top, rolled[SUBLANES:]], axis=0)


def _mod_kernel(c_ref, w_ref, b_ref, o_ref):
    c = c_ref[...]
    s = c * jax.nn.sigmoid(c)
    o_ref[...] = jnp.dot(s, w_ref[...], preferred_element_type=F32,
                         precision=lax.Precision.HIGHEST) + b_ref[...]


def _modulation(c, w_ada, b_ada, tn=512):
    bsz, d = c.shape
    n = w_ada.shape[1]
    rows = -(-bsz // SUBLANES) * SUBLANES
    c_pad = jnp.zeros((rows, d), F32).at[:bsz].set(c)
    out = pl.pallas_call(
        _mod_kernel,
        out_shape=jax.ShapeDtypeStruct((rows, n), F32),
        grid=(n // tn,),
        in_specs=[pl.BlockSpec((rows, d), lambda j: (0, 0)),
                  pl.BlockSpec((d, tn), lambda j: (0, j)),
                  pl.BlockSpec((1, tn), lambda j: (0, j))],
        out_specs=pl.BlockSpec((rows, tn), lambda j: (0, j)),
        compiler_params=_cparams(("arbitrary",)),
        name="modulation",
    )(c_pad, w_ada, b_ada.reshape(1, n))
    return out[:bsz]


def _inproj_kernel(x_ref, g_ref, sc_ref, sh_ref, w_ref, ws_ref, o_ref, os_ref, h_ref):
    @pl.when(pl.program_id(2) == 0)
    def _():
        x = x_ref[...]
        y = x * _rms_scale(x) * g_ref[...]
        h = (y * (1.0 + sc_ref[...]) + sh_ref[...]).astype(BF16)
        h_ref[...] = h
        os_ref[...] = jnp.dot(h, ws_ref[...], preferred_element_type=F32)

    o_ref[...] = jnp.dot(h_ref[...], w_ref[...], preferred_element_type=F32).astype(BF16)


def _in_proj(x, g, sc, sh, w_main, n, w_small, bm, bn):
    bsz, s, d = x.shape
    ns = w_small.shape[1]
    return pl.pallas_call(
        _inproj_kernel,
        out_shape=(jax.ShapeDtypeStruct((bsz, s, n), BF16), jax.ShapeDtypeStruct((bsz, s, ns), F32)),
        grid=(bsz, s // bm, n // bn),
        in_specs=[pl.BlockSpec((None, bm, d), lambda b, i, j: (b, i, 0)),
                  pl.BlockSpec((1, d), lambda b, i, j: (0, 0)),
                  pl.BlockSpec((None, 1, d), lambda b, i, j: (b, 0, 0)),
                  pl.BlockSpec((None, 1, d), lambda b, i, j: (b, 0, 0)),
                  pl.BlockSpec((d, bn), lambda b, i, j: (0, j)),
                  pl.BlockSpec((d, ns), lambda b, i, j: (0, 0))],
        out_specs=(pl.BlockSpec((None, bm, bn), lambda b, i, j: (b, i, j)),
                   pl.BlockSpec((None, bm, ns), lambda b, i, j: (b, i, 0))),
        scratch_shapes=[pltpu.VMEM((bm, d), BF16)],
        compiler_params=_cparams(("arbitrary", "arbitrary", "arbitrary")),
        name="in_proj",
    )(x, g, sc, sh, w_main, w_small)


def _rglru_kernel(xl_ref, gate_ref, cw_ref, cb_ref, wa_ref, ba_ref, wx_ref, bx_ref, lam_ref, g_ref,
                  o_ref, halo_ref, hc_ref, a_scr, b_scr):
    bt, dl = xl_ref.shape
    nblk = dl // LRU_BLOCK

    @pl.when(pl.program_id(1) == 0)
    def _():
        halo_ref[...] = jnp.zeros_like(halo_ref)
        hc_ref[...] = jnp.zeros_like(hc_ref)

    xl = xl_ref[...].astype(F32)
    prev = halo_ref[...]
    halo_ref[...] = xl[bt - SUBLANES:, :]

    cw = cw_ref[...]
    kw = cw.shape[0]
    xc = cb_ref[...] + cw[kw - 1:kw, :] * xl
    for dlt in range(1, kw):
        xc = xc + cw[kw - 1 - dlt:kw - dlt, :] * _shift_rows(xl, prev, dlt)

    xcb = xc.astype(BF16)
    r_parts, i_parts = [], []
    for n in range(nblk):
        blk = xcb[:, n * LRU_BLOCK:(n + 1) * LRU_BLOCK]
        r_parts.append(jnp.dot(blk, wa_ref[n], preferred_element_type=F32))
        i_parts.append(jnp.dot(blk, wx_ref[n], preferred_element_type=F32))
    r = jax.nn.sigmoid(jnp.concatenate(r_parts, axis=1) + ba_ref[...])
    ig = jax.nn.sigmoid(jnp.concatenate(i_parts, axis=1) + bx_ref[...])

    z = -lam_ref[...]
    softplus = jnp.maximum(z, 0.0) + jnp.log(1.0 + jnp.exp(-jnp.abs(z)))
    log_a = (-LRU_C * softplus) * r
    a = jnp.exp(log_a)
    u = jnp.sqrt(1.0 - jnp.exp(2.0 * log_a)) * (ig * xc)

    rowmod = lax.broadcasted_iota(I32, (bt, dl), 0) % SUBLANES
    b = u
    for sft in (1, 2, 4):
        keep = rowmod >= sft
        a_prev = jnp.where(keep, pltpu.roll(a, sft, 0), 1.0)
        b_prev = jnp.where(keep, pltpu.roll(b, sft, 0), 0.0)
        b = a * b_prev + b
        a = a * a_prev
    a_scr[...] = a
    b_scr[...] = b

    def group(gi, carry):
        rows = pl.ds(pl.multiple_of(gi * SUBLANES, SUBLANES), SUBLANES)
        h = a_scr[rows, :] * carry + b_scr[rows, :]
        b_scr[rows, :] = h
        return jnp.broadcast_to(h[SUBLANES - 1:SUBLANES, :], (SUBLANES, dl))

    hc_ref[...] = lax.fori_loop(0, bt // SUBLANES, group, hc_ref[...])

    y = b_scr[...] * _gelu_tanh(gate_ref[...].astype(F32))
    o_ref[...] = (y * _rms_scale(y) * g_ref[...]).astype(BF16)


def _rglru(proj, x_off, cw, cb, wa, ba, wx, bx, lam, g, bt):
    bsz, s, _ = proj.shape
    dl = cw.shape[1]
    nblk = dl // LRU_BLOCK
    x_blk = x_off // dl
    vec = lambda: pl.BlockSpec((1, dl), lambda b, t: (0, 0))
    return pl.pallas_call(
        _rglru_kernel,
        out_shape=jax.ShapeDtypeStruct((bsz, s, dl), BF16),
        grid=(bsz, s // bt),
        in_specs=[pl.BlockSpec((None, bt, dl), lambda b, t: (b, t, x_blk)),
                  pl.BlockSpec((None, bt, dl), lambda b, t: (b, t, x_blk + 1)),
                  pl.BlockSpec((cw.shape[0], dl), lambda b, t: (0, 0)),
                  vec(),
                  pl.BlockSpec((nblk, LRU_BLOCK, LRU_BLOCK), lambda b, t: (0, 0, 0)),
                  vec(),
                  pl.BlockSpec((nblk, LRU_BLOCK, LRU_BLOCK), lambda b, t: (0, 0, 0)),
                  vec(), vec(), vec()],
        out_specs=pl.BlockSpec((None, bt, dl), lambda b, t: (b, t, 0)),
        scratch_shapes=[pltpu.VMEM((SUBLANES, dl), F32), pltpu.VMEM((SUBLANES, dl), F32),
                        pltpu.VMEM((bt, dl), F32), pltpu.VMEM((bt, dl), F32)],
        compiler_params=_cparams(("arbitrary", "arbitrary")),
        name="rglru",
    )(proj, proj, cw, cb, wa, ba, wx, bx, lam, g)


def _dsa_kernel(q_ref, *refs, k_sel, rep, n_qi):
    qi_refs = refs[:n_qi]
    (k_ref, v_ref, ki_ref, w_ref, g_ref, o_ref,
     key_scr, bias_scr, m_scr, l_scr, acc_scr, out_scr) = refs[n_qi:]
    bq = q_ref.shape[0]
    nkb_all, kb_sz, _ = key_scr.shape
    n_groups = q_ref.shape[1] // (rep * HEAD_DIM)
    qblk = pl.program_id(1)
    nkb = lax.div((qblk + 1) * bq + kb_sz - 1, kb_sz)
    qi_w = qi_refs[0].shape[1]
    n_idx_heads = n_qi * qi_w // IDX_DIM
    lane_tiles = kb_sz // LANES
    tile = lambda v: jnp.concatenate([v] * lane_tiles, axis=1)
    down = lambda v: jnp.concatenate([v] * (kb_sz // SUBLANES), axis=0)

    w_t = jnp.transpose(w_ref[...])
    w_off = IDX_DIM
    q_heads = []
    for h in range(n_idx_heads):
        piece, col = divmod(h * IDX_DIM, qi_w)
        q_heads.append(qi_refs[piece][:, col:col + IDX_DIM])

    def score_block(kb, carry):
        rows = pl.ds(pl.multiple_of(kb * kb_sz, kb_sz), kb_sz)
        ki = ki_ref[rows, 0:IDX_DIM].astype(BF16)
        acc = jnp.zeros((kb_sz, bq), F32)
        for h, qh in enumerate(q_heads):
            dots = lax.dot_general(ki, qh, (((1,), (1,)), ((), ())), preferred_element_type=F32)
            acc = acc + jnp.maximum(dots, 0.0) * w_t[w_off + h:w_off + h + 1, :]
        bits = lax.bitcast_convert_type(acc, I32)
        key = jnp.where(bits < 0, bits ^ jnp.int32(0x7FFFFFFF), bits)
        kpos = kb * kb_sz + lax.broadcasted_iota(I32, (kb_sz, bq), 0)
        qpos = qblk * bq + lax.broadcasted_iota(I32, (kb_sz, bq), 1)
        key_scr[kb] = jnp.where(kpos <= qpos, key, jnp.int32(INT_MIN))
        return carry

    lax.fori_loop(0, nkb, score_block, 0)

    def count_keys(hit_fn):
        lanes = COUNT_CHAINS

        def body(kb, c):
            ones = jnp.where(hit_fn(kb), 1.0, 0.0)
            return c + jnp.sum(ones.reshape(lanes, kb_sz // (lanes * SUBLANES), SUBLANES, bq), axis=1)
        c = lax.fori_loop(0, nkb, body, jnp.zeros((lanes, SUBLANES, bq), F32))
        return jnp.broadcast_to(jnp.sum(c.reshape(lanes * SUBLANES, bq), axis=0, keepdims=True), (SUBLANES, bq))

    def count_ge(cand):
        cand_rows = down(cand)
        return count_keys(lambda kb: key_scr[kb] >= cand_rows)

    def bit_step(it, carry):
        ans, n_ge = carry
        cand_u = ans | lax.shift_left(jnp.int32(1), 31 - it)
        cnt = count_ge(cand_u ^ jnp.int32(INT_MIN))
        ok = cnt >= k_sel
        return jnp.where(ok, cand_u, ans), jnp.where(ok, cnt, n_ge)

    ans, n_ge = lax.fori_loop(0, 32, bit_step,
                              (jnp.zeros((SUBLANES, bq), I32), jnp.zeros((SUBLANES, bq), F32)))
    thr = jnp.maximum(ans ^ jnp.int32(INT_MIN), jnp.int32(INT_MIN + 1))
    thr_rows = down(thr)

    @pl.when(jnp.max(n_ge) > k_sel)
    def _():
        need = k_sel - count_ge(thr + 1)
        pos_bits = (nkb_all * kb_sz - 1).bit_length()

        def ties_before(pos):
            pos_rows = down(pos)

            def hit(kb):
                kpos = kb * kb_sz + lax.broadcasted_iota(I32, (kb_sz, bq), 0)
                return (key_scr[kb] == thr_rows) & (kpos < pos_rows)
            return count_keys(hit)

        def pos_step(it, last):
            cand = last | lax.shift_left(jnp.int32(1), pos_bits - 1 - it)
            return jnp.where(ties_before(cand) < need, cand, last)

        last_rows = down(lax.fori_loop(0, pos_bits, pos_step, jnp.zeros((SUBLANES, bq), I32)))

        def demote(kb, carry):
            key = key_scr[kb]
            kpos = kb * kb_sz + lax.broadcasted_iota(I32, (kb_sz, bq), 0)
            key_scr[kb] = jnp.where((key == thr_rows) & (kpos > last_rows), key - 1, key)
            return carry

        lax.fori_loop(0, nkb, demote, 0)

    def to_bias(kb, carry):
        bias = jnp.where(key_scr[kb] >= thr_rows, 0.0, NEG_BIG).astype(F32)
        bias_scr[kb] = jnp.transpose(bias)
        return carry

    lax.fori_loop(0, nkb, to_bias, 0)

    c_exp = (HEAD_DIM ** -0.5) * LOG2_E
    sweep = m_scr.shape[0]
    for g0 in range(0, n_groups, sweep):
        heads = range(g0, g0 + sweep)
        qss = [jnp.concatenate(
            [q_ref[:, (g * rep + r) * HEAD_DIM:(g * rep + r + 1) * HEAD_DIM] for r in range(rep)], axis=0)
            for g in heads]
        m_scr[...] = jnp.full_like(m_scr, NEG_BIG)
        l_scr[...] = jnp.zeros_like(l_scr)
        acc_scr[...] = jnp.zeros_like(acc_scr)

        def kv_block(kb, carry):
            rows = pl.ds(pl.multiple_of(kb * kb_sz, kb_sz), kb_sz)
            bias = jnp.concatenate([bias_scr[kb]] * rep, axis=0)
            logits = [lax.dot_general(qs, k_ref[rows, g * HEAD_DIM:(g + 1) * HEAD_DIM],
                                      (((1,), (1,)), ((), ())), preferred_element_type=F32)
                      for qs, g in zip(qss, heads)]
            for slot, (s, g) in enumerate(zip(logits, heads)):
                s = s + bias
                m_prev = m_scr[slot]
                m_new = jnp.maximum(m_prev, jnp.max(s, axis=1, keepdims=True))
                alpha = jnp.exp2((m_prev - m_new) * c_exp)
                p = jnp.exp2((s - tile(m_new)) * c_exp)
                l_scr[slot] = alpha * l_scr[slot] + jnp.sum(p, axis=1, keepdims=True)
                acc_scr[slot] = alpha * acc_scr[slot] + jnp.dot(
                    p.astype(BF16), v_ref[rows, g * HEAD_DIM:(g + 1) * HEAD_DIM], preferred_element_type=F32)
                m_scr[slot] = m_new
            return carry

        lax.fori_loop(0, nkb, kv_block, 0)
        for slot, g in enumerate(heads):
            o = acc_scr[slot] / l_scr[slot]
            for r in range(rep):
                out_scr[:, (g * rep + r) * HEAD_DIM:(g * rep + r + 1) * HEAD_DIM] = o[r * bq:(r + 1) * bq, :]

    y = out_scr[...]
    o_ref[...] = (y * _rms_scale(y) * g_ref[...]).astype(BF16)


def _dsa(proj, q_off, qi_off, k_off, small, g, d_attn, d_kv, bq, kb_sz):
    bsz, s, _ = proj.shape
    nkb_all = s // kb_sz
    rep = d_attn // d_kv
    k_sel = min(TOPK_MAX, s // 4)
    d_qi = N_IDX_HEADS * IDX_DIM
    qi_w = math.gcd(qi_off, d_qi)
    n_qi = d_qi // qi_w
    q_blk = q_off // d_attn
    k_blk = k_off // d_kv
    kern = functools.partial(_dsa_kernel, k_sel=k_sel, rep=rep, n_qi=n_qi)
    qi_specs = [pl.BlockSpec((None, bq, qi_w), lambda b, i, t=t: (b, i, qi_off // qi_w + t)) for t in range(n_qi)]
    return pl.pallas_call(
        kern,
        out_shape=jax.ShapeDtypeStruct((bsz, s, d_attn), BF16),
        grid=(bsz, s // bq),
        in_specs=[pl.BlockSpec((None, bq, d_attn), lambda b, i: (b, i, q_blk))] + qi_specs + [
                  pl.BlockSpec((None, s, d_kv), lambda b, i: (b, 0, k_blk)),
                  pl.BlockSpec((None, s, d_kv), lambda b, i: (b, 0, k_blk + 1)),
                  pl.BlockSpec((None, s, LANES), lambda b, i: (b, 0, 0)),
                  pl.BlockSpec((None, bq, LANES), lambda b, i: (b, i, 0)),
                  pl.BlockSpec((1, d_attn), lambda b, i: (0, 0))],
        out_specs=pl.BlockSpec((None, bq, d_attn), lambda b, i: (b, i, 0)),
        scratch_shapes=[pltpu.VMEM((nkb_all, kb_sz, bq), I32),
                        pltpu.VMEM((nkb_all, bq, kb_sz), F32),
                        pltpu.VMEM((ATTN_HEADS_PER_SWEEP, rep * bq, LANES), F32),
                        pltpu.VMEM((ATTN_HEADS_PER_SWEEP, rep * bq, LANES), F32),
                        pltpu.VMEM((ATTN_HEADS_PER_SWEEP, rep * bq, HEAD_DIM), F32),
                        pltpu.VMEM((bq, d_attn), F32)],
        compiler_params=_cparams(("arbitrary", "arbitrary")),
        name="dsa",
    )(proj, *([proj] * n_qi), proj, proj, small, small, g)


def _outproj_kernel(a1_ref, a2_ref, w_ref, x_ref, gpost_ref, gt_ref, gpre_ref, sc_ref, sh_ref,
                    x1_ref, h2_ref, mix_ref):
    j = pl.program_id(2)
    n_tiles, bm, tn = mix_ref.shape
    k1 = a1_ref.shape[1]
    mix_ref[j] = (jnp.dot(a1_ref[...], w_ref[0:k1, :], preferred_element_type=F32)
                  + jnp.dot(a2_ref[...], w_ref[k1:, :], preferred_element_type=F32))

    @pl.when(j == n_tiles - 1)
    def _():
        post = gt_ref[...] * gpost_ref[...]
        pre = gpre_ref[...] * (1.0 + sc_ref[...])
        sh = sh_ref[...]

        def rows_out(r, carry):
            rows = pl.ds(pl.multiple_of(r * SUBLANES, SUBLANES), SUBLANES)
            m = jnp.concatenate([mix_ref[t, rows, :] for t in range(n_tiles)], axis=1)
            x1 = x_ref[rows, :] + post * (m * _rms_scale(m))
            x1_ref[rows, :] = x1
            h2_ref[rows, :] = ((x1 * _rms_scale(x1)) * pre + sh).astype(BF16)
            return carry

        lax.fori_loop(0, bm // SUBLANES, rows_out, 0, unroll=8)


def _out_proj(a1, a2, w, x, gpost, gt, gpre, sc, sh, bm, tn):
    bsz, s, d = x.shape
    k1, k2 = a1.shape[2], a2.shape[2]
    n_tiles = d // tn
    vec = lambda: pl.BlockSpec((1, d), lambda b, i, j: (0, 0))
    bvec = lambda: pl.BlockSpec((None, 1, d), lambda b, i, j: (b, 0, 0))
    big = lambda bufs=None: pl.BlockSpec((None, bm, d), lambda b, i, j: (b, i, 0), pipeline_mode=bufs)
    return pl.pallas_call(
        _outproj_kernel,
        out_shape=(jax.ShapeDtypeStruct((bsz, s, d), F32), jax.ShapeDtypeStruct((bsz, s, d), BF16)),
        grid=(bsz, s // bm, n_tiles),
        in_specs=[pl.BlockSpec((None, bm, k1), lambda b, i, j: (b, i, 0), pipeline_mode=pl.Buffered(1)),
                  pl.BlockSpec((None, bm, k2), lambda b, i, j: (b, i, 0), pipeline_mode=pl.Buffered(1)),
                  pl.BlockSpec((k1 + k2, tn), lambda b, i, j: (0, j)),
                  big(), vec(), bvec(), vec(), bvec(), bvec()],
        out_specs=(big(), big(pl.Buffered(1))),
        scratch_shapes=[pltpu.VMEM((n_tiles, bm, tn), F32)],
        compiler_params=_cparams(("arbitrary", "arbitrary", "arbitrary")),
        name="out_proj",
    )(a1, a2, w, x, gpost, gt, gpre, sc, sh)


FFN_ROW_CHUNK = 64
FFN_DOWN_PIECES = 8


def _ffn_kernel(h_ref, wg_ref, wu_ref, cwg_ref, cwu_ref, cbg_ref, cbu_ref, wo_ref, x_ref, g_ref, gt_ref,
                o_ref, u_even, u_odd, act_even, act_odd, carry_ref):
    i = pl.program_id(1)
    s = pl.program_id(2)
    nj = pl.num_programs(2) - 2
    bm, d = h_ref.shape
    tf = wo_ref.shape[0]
    tile = jnp.minimum(s, nj - 1)
    u_bufs = (u_even, u_odd)
    act_bufs = (act_even, act_odd)

    @pl.when(s == 0)
    def _():
        o_ref[...] = jnp.zeros_like(o_ref)

    @pl.when(jnp.logical_and(i == 0, s < nj))
    def _():
        carry_ref[tile] = jnp.zeros(carry_ref.shape[1:], F32)

    def step(p, up_proj, finish, down_proj):
        u_new, u_last = u_bufs[p], u_bufs[1 - p]
        act_new, act_last = act_bufs[1 - p], act_bufs[p]
        cwg, cwu, cbg, cbu = cwg_ref[...], cwu_ref[...], cbg_ref[...], cbu_ref[...]

        def conv(r0, c0, cw, cb):
            cols = slice(c0, c0 + tf)
            return (cb + cw[2:3, :] * u_last[r0:r0 + FFN_ROW_CHUNK, cols]
                    + cw[1:2, :] * u_last[r0 - 1:r0 - 1 + FFN_ROW_CHUNK, cols]
                    + cw[0:1, :] * u_last[r0 - 2:r0 - 2 + FFN_ROW_CHUNK, cols])

        def finish_rows(c, after):
            if after is None:
                zero = 0.0
            else:
                bits = lax.bitcast_convert_type(after[0:1, 0:tf], jnp.uint32)
                zero = lax.shift_right_logical(lax.shift_right_logical(bits, jnp.uint32(16)),
                                               jnp.uint32(16)).astype(F32)
            r0 = SUBLANES + c * FFN_ROW_CHUNK
            gate = conv(r0, 0, cwg, cbg + zero)
            up = conv(r0, tf, cwu, cbu + zero)
            act_new[c * FFN_ROW_CHUNK:(c + 1) * FFN_ROW_CHUNK, :] = (_gelu_tanh(gate) * up).astype(BF16)

        n_chunks = bm // FFN_ROW_CHUNK
        n_pieces = min(FFN_DOWN_PIECES, n_chunks)
        per_piece = n_chunks // n_pieces
        assert per_piece * n_pieces == n_chunks
        nw = d // n_pieces
        act_l = act_last[...] if down_proj else None
        for pc in range(n_pieces):
            piece = None
            if down_proj:
                cols = slice(pc * nw, (pc + 1) * nw)
                piece = jnp.dot(act_l, wo_ref[:, cols], preferred_element_type=F32)
                o_ref[:, cols] += piece
            if finish:
                for c in range(pc * per_piece, (pc + 1) * per_piece):
                    finish_rows(c, piece)

        if up_proj:
            w_gu = jnp.concatenate([wg_ref[...], wu_ref[...]], axis=1)
            u = jnp.dot(h_ref[...], w_gu, preferred_element_type=F32)
            u_new[0:SUBLANES, :] = carry_ref[tile]
            u_new[SUBLANES:, :] = u
            carry_ref[tile] = u[bm - SUBLANES:, :]

    parity = lax.rem(s, 2)
    steady = jnp.logical_and(s >= 2, s < nj)
    variants = [(s == 0, 0, (True, False, False)),
                (s == 1, 1, (True, True, False)),
                (jnp.logical_and(steady, parity == 0), 0, (True, True, True)),
                (jnp.logical_and(steady, parity == 1), 1, (True, True, True)),
                (s == nj, nj % 2, (False, True, True)),
                (s == nj + 1, (nj + 1) % 2, (False, False, True))]
    for cond, p, stages in variants:
        pl.when(cond)(functools.partial(step, p, *stages))

    @pl.when(s == nj + 1)
    def _():
        scale = gt_ref[...] * g_ref[...]
        for r0 in range(0, bm, SUBLANES):
            f = o_ref[r0:r0 + SUBLANES, :]
            o_ref[r0:r0 + SUBLANES, :] = x_ref[r0:r0 + SUBLANES, :] + scale * (f * _rms_scale(f))


def _conv_ffn(h, w_in, cw, cb, wo, x, g, gt, bm, tf):
    bsz, s, d = x.shape
    nj = wo.shape[0] // tf
    kw = cw.shape[0]
    up_t = lambda j: jnp.minimum(j, nj - 1)
    ew_t = lambda j: jnp.clip(j - 1, 0, nj - 1)
    dn_t = lambda j: jnp.clip(j - 2, 0, nj - 1)
    big = lambda: pl.BlockSpec((None, bm, d), lambda b, i, j: (b, i, 0))
    return pl.pallas_call(
        _ffn_kernel,
        out_shape=jax.ShapeDtypeStruct((bsz, s, d), F32),
        grid=(bsz, s // bm, nj + 2),
        in_specs=[pl.BlockSpec((None, bm, d), lambda b, i, j: (b, i, 0), pipeline_mode=pl.Buffered(1)),
                  pl.BlockSpec((d, tf), lambda b, i, j: (0, up_t(j))),
                  pl.BlockSpec((d, tf), lambda b, i, j: (0, nj + up_t(j))),
                  pl.BlockSpec((kw, tf), lambda b, i, j: (0, ew_t(j))),
                  pl.BlockSpec((kw, tf), lambda b, i, j: (0, nj + ew_t(j))),
                  pl.BlockSpec((1, tf), lambda b, i, j: (0, ew_t(j))),
                  pl.BlockSpec((1, tf), lambda b, i, j: (0, nj + ew_t(j))),
                  pl.BlockSpec((tf, d), lambda b, i, j: (dn_t(j), 0)),
                  big(),
                  pl.BlockSpec((1, d), lambda b, i, j: (0, 0)),
                  pl.BlockSpec((None, 1, d), lambda b, i, j: (b, 0, 0))],
        out_specs=big(),
        scratch_shapes=[pltpu.VMEM((SUBLANES + bm, 2 * tf), F32),
                        pltpu.VMEM((SUBLANES + bm, 2 * tf), F32),
                        pltpu.VMEM((bm, tf), BF16),
                        pltpu.VMEM((bm, tf), BF16),
                        pltpu.VMEM((nj, SUBLANES, 2 * tf), F32)],
        compiler_params=_cparams(("arbitrary", "arbitrary", "arbitrary")),
        name="conv_ffn",
    )(h, w_in, w_in, cw, cw, cb, cb, wo, x, g, gt)


def _layer(x, c, w_ada, b_ada, g_pre_mix, g_post_mix, g_pre_ffn, g_post_ffn, w_in,
           conv_lru_w, conv_lru_b, w_rg_a, b_rg_a, w_rg_x, b_rg_x, lru_lambda,
           g_grp_lru, g_grp_attn, w_out, w_ffn_in, conv_ffn_w, conv_ffn_b, w_ffn_out,
           *, bm_in, bn_in, bt_lru, bq, kb_sz, bm_out, tn_out, bm_ffn, tf):
    bsz, s, d = x.shape
    d_lru = w_rg_a.shape[0] * LRU_BLOCK
    d_attn = g_grp_attn.shape[0]
    d_qi = N_IDX_HEADS * IDX_DIM
    d_kv = (w_in.shape[1] - 2 * d_lru - d_attn - d_qi - IDX_DIM - N_IDX_HEADS) // 2
    row = lambda v: v.reshape(1, -1)

    mod = _modulation(c, w_ada, b_ada)
    sh1, sc1, gt1, sh2, sc2, gt2 = [m.reshape(bsz, 1, d) for m in jnp.split(mod, 6, axis=-1)]

    q_off = 2 * d_lru
    k_off = q_off + d_attn
    qi_off = k_off + 2 * d_kv
    n_main = qi_off + d_qi
    n_small = w_in.shape[1] - n_main
    w_in_b = w_in.astype(BF16)
    w_small = jnp.concatenate([w_in[:, n_main:], jnp.zeros((d, LANES - n_small), F32)], axis=1).astype(BF16)

    proj, small = _in_proj(x, row(g_pre_mix), sc1, sh1, w_in_b, n_main, w_small, bm_in, bn_in)

    y_lru = _rglru(proj, 0, conv_lru_w, row(conv_lru_b), w_rg_a.astype(BF16), row(b_rg_a),
                   w_rg_x.astype(BF16), row(b_rg_x), row(lru_lambda), row(g_grp_lru), bt_lru)

    y_attn = _dsa(proj, q_off, qi_off, k_off, small, row(g_grp_attn), d_attn, d_kv, bq, kb_sz)

    x1, h2 = _out_proj(y_lru, y_attn, w_out.astype(BF16), x, row(g_post_mix), gt1, row(g_pre_ffn), sc2, sh2,
                       bm_out, tn_out)

    d_ff = w_ffn_out.shape[0]
    pad = -d_ff % tf
    halves = lambda a: jnp.concatenate([jnp.pad(a[:, :d_ff], ((0, 0), (0, pad))),
                                        jnp.pad(a[:, d_ff:], ((0, 0), (0, pad)))], axis=1)
    return _conv_ffn(h2, halves(w_ffn_in).astype(BF16), halves(conv_ffn_w), halves(row(conv_ffn_b)),
                     jnp.pad(w_ffn_out, ((0, pad), (0, 0))).astype(BF16), x1, row(g_post_ffn), gt2, bm_ffn, tf)


_TILES = dict(bm_in=512, bn_in=1024, bt_lru=256, bq=256, kb_sz=512, bm_out=512, tn_out=512, bm_ffn=512, tf=256)


def kernel(x, c, w_ada, b_ada, g_pre_mix, g_post_mix, g_pre_ffn, g_post_ffn, w_in, conv_lru_w, conv_lru_b,
           w_rg_a, b_rg_a, w_rg_x, b_rg_x, lru_lambda, g_grp_lru, g_grp_attn, w_out, w_ffn_in,
           conv_ffn_w, conv_ffn_b, w_ffn_out):
    for l in range(w_ada.shape[0]):
        x = _layer(x, c, w_ada[l], b_ada[l], g_pre_mix[l], g_post_mix[l], g_pre_ffn[l], g_post_ffn[l],
                   w_in[l], conv_lru_w[l], conv_lru_b[l], w_rg_a[l], b_rg_a[l], w_rg_x[l], b_rg_x[l],
                   lru_lambda[l], g_grp_lru[l], g_grp_attn[l], w_out[l], w_ffn_in[l], conv_ffn_w[l],
                   conv_ffn_b[l], w_ffn_out[l], **_TILES)
    return x
```

```python
import functools
import math

import jax
import jax.numpy as jnp
from jax import lax
from jax.experimental import pallas as pl
from jax.experimental.pallas import tpu as pltpu

F32 = jnp.float32
BF16 = jnp.bfloat16
I32 = jnp.int32

RMS_EPS = 1e-6
LRU_C = 8.0
LRU_BLOCK = 128
HEAD_DIM = 128
N_KV_HEADS = 4
N_IDX_HEADS = 32
IDX_DIM = 64
TOPK_MAX = 256

LANES = 128
SUBLANES = 8
VMEM_LIMIT = 59 * 1024 * 1024

ATTN_HEADS_PER_SWEEP = 2
COUNT_CHAINS = 2
INT_MIN = -(2 ** 31)
NEG_BIG = -1e30
LOG2_E = 1.4426950408889634


def _cparams(sem):
    return pltpu.CompilerParams(dimension_semantics=sem, vmem_limit_bytes=VMEM_LIMIT)


def _rms_scale(v):
    return lax.rsqrt(jnp.mean(v * v, axis=-1, keepdims=True) + RMS_EPS)


def _gelu_tanh(v):
    inner = v * (1.0 + 0.044715 * (v * v))
    return v / (1.0 + jnp.exp((-2.0 * 0.7978845608028654) * inner))


def _shift_rows(cur, prev8, d):
    rolled = pltpu.roll(cur, d, 0)
    head = pltpu.roll(prev8, d, 0)
    row = lax.broadcasted_iota(I32, (SUBLANES, cur.shape[1]), 0)
    top = jnp.where(row < d, head, rolled[0:SUBLANES])
    return jnp.concatenate([top, rolled[SUBLANES:]], axis=0)


def _mod_kernel(c_ref, w_ref, b_ref, o_ref):
    c = c_ref[...]
    s = c * jax.nn.sigmoid(c)
    o_ref[...] = jnp.dot(s, w_ref[...], preferred_element_type=F32,
                         precision=lax.Precision.HIGHEST) + b_ref[...]


def _modulation(c, w_ada, b_ada, tn=512):
    bsz, d = c.shape
    n = w_ada.shape[1]
    rows = -(-bsz // SUBLANES) * SUBLANES
    c_pad = jnp.zeros((rows, d), F32).at[:bsz].set(c)
    out = pl.pallas_call(
        _mod_kernel,
        out_shape=jax.ShapeDtypeStruct((rows, n), F32),
        grid=(n // tn,),
        in_specs=[pl.BlockSpec((rows, d), lambda j: (0, 0)),
                  pl.BlockSpec((d, tn), lambda j: (0, j)),
                  pl.BlockSpec((1, tn), lambda j: (0, j))],
        out_specs=pl.BlockSpec((rows, tn), lambda j: (0, j)),
        compiler_params=_cparams(("arbitrary",)),
        name="modulation",
    )(c_pad, w_ada, b_ada.reshape(1, n))
    return out[:bsz]


def _inproj_kernel(x_ref, g_ref, sc_ref, sh_ref, w_ref, ws_ref, o_ref, os_ref, h_ref):
    @pl.when(pl.program_id(2) == 0)
    def _():
        x = x_ref[...]
        y = x * _rms_scale(x) * g_ref[...]
        h = (y * (1.0 + sc_ref[...]) + sh_ref[...]).astype(BF16)
        h_ref[...] = h
        os_ref[...] = jnp.dot(h, ws_ref[...], preferred_element_type=F32)

    o_ref[...] = jnp.dot(h_ref[...], w_ref[...], preferred_element_type=F32).astype(BF16)


def _in_proj(x, g, sc, sh, w_main, n, w_small, bm, bn):
    bsz, s, d = x.shape
    ns = w_small.shape[1]
    return pl.pallas_call(
        _inproj_kernel,
        out_shape=(jax.ShapeDtypeStruct((bsz, s, n), BF16), jax.ShapeDtypeStruct((bsz, s, ns), F32)),
        grid=(bsz, s // bm, n // bn),
        in_specs=[pl.BlockSpec((None, bm, d), lambda b, i, j: (b, i, 0)),
                  pl.BlockSpec((1, d), lambda b, i, j: (0, 0)),
                  pl.BlockSpec((None, 1, d), lambda b, i, j: (b, 0, 0)),
                  pl.BlockSpec((None, 1, d), lambda b, i, j: (b, 0, 0)),
                  pl.BlockSpec((d, bn), lambda b, i, j: (0, j)),
                  pl.BlockSpec((d, ns), lambda b, i, j: (0, 0))],
        out_specs=(pl.BlockSpec((None, bm, bn), lambda b, i, j: (b, i, j)),
                   pl.BlockSpec((None, bm, ns), lambda b, i, j: (b, i, 0))),
        scratch_shapes=[pltpu.VMEM((bm, d), BF16)],
        compiler_params=_cparams(("arbitrary", "arbitrary", "arbitrary")),
        name="in_proj",
    )(x, g, sc, sh, w_main, w_small)


def _rglru_kernel(xl_ref, gate_ref, cw_ref, cb_ref, wa_ref, ba_ref, wx_ref, bx_ref, lam_ref, g_ref,
                  o_ref, halo_ref, hc_ref, a_scr, b_scr):
    bt, dl = xl_ref.shape
    nblk = dl // LRU_BLOCK

    @pl.when(pl.program_id(1) == 0)
    def _():
        halo_ref[...] = jnp.zeros_like(halo_ref)
        hc_ref[...] = jnp.zeros_like(hc_ref)

    xl = xl_ref[...].astype(F32)
    prev = halo_ref[...]
    halo_ref[...] = xl[bt - SUBLANES:, :]

    cw = cw_ref[...]
    kw = cw.shape[0]
    xc = cb_ref[...] + cw[kw - 1:kw, :] * xl
    for dlt in range(1, kw):
        xc = xc + cw[kw - 1 - dlt:kw - dlt, :] * _shift_rows(xl, prev, dlt)

    xcb = xc.astype(BF16)
    r_parts, i_parts = [], []
    for n in range(nblk):
        blk = xcb[:, n * LRU_BLOCK:(n + 1) * LRU_BLOCK]
        r_parts.append(jnp.dot(blk, wa_ref[n], preferred_element_type=F32))
        i_parts.append(jnp.dot(blk, wx_ref[n], preferred_element_type=F32))
    r = jax.nn.sigmoid(jnp.concatenate(r_parts, axis=1) + ba_ref[...])
    ig = jax.nn.sigmoid(jnp.concatenate(i_parts, axis=1) + bx_ref[...])

    z = -lam_ref[...]
    softplus = jnp.maximum(z, 0.0) + jnp.log(1.0 + jnp.exp(-jnp.abs(z)))
    log_a = (-LRU_C * softplus) * r
    a = jnp.exp(log_a)
    u = jnp.sqrt(1.0 - jnp.exp(2.0 * log_a)) * (ig * xc)

    rowmod = lax.broadcasted_iota(I32, (bt, dl), 0) % SUBLANES
    b = u
    for sft in (1, 2, 4):
        keep = rowmod >= sft
        a_prev = jnp.where(keep, pltpu.roll(a, sft, 0), 1.0)
        b_prev = jnp.where(keep, pltpu.roll(b, sft, 0), 0.0)
        b = a * b_prev + b
        a = a * a_prev
    a_scr[...] = a
    b_scr[...] = b

    def group(gi, carry):
        rows = pl.ds(pl.multiple_of(gi * SUBLANES, SUBLANES), SUBLANES)
        h = a_scr[rows, :] * carry + b_scr[rows, :]
        b_scr[rows, :] = h
        return jnp.broadcast_to(h[SUBLANES - 1:SUBLANES, :], (SUBLANES, dl))

    hc_ref[...] = lax.fori_loop(0, bt // SUBLANES, group, hc_ref[...])

    y = b_scr[...] * _gelu_tanh(gate_ref[...].astype(F32))
    o_ref[...] = (y * _rms_scale(y) * g_ref[...]).astype(BF16)


def _rglru(proj, x_off, cw, cb, wa, ba, wx, bx, lam, g, bt):
    bsz, s, _ = proj.shape
    dl = cw.shape[1]
    nblk = dl // LRU_BLOCK
    x_blk = x_off // dl
    vec = lambda: pl.BlockSpec((1, dl), lambda b, t: (0, 0))
    return pl.pallas_call(
        _rglru_kernel,
        out_shape=jax.ShapeDtypeStruct((bsz, s, dl), BF16),
        grid=(bsz, s // bt),
        in_specs=[pl.BlockSpec((None, bt, dl), lambda b, t: (b, t, x_blk)),
                  pl.BlockSpec((None, bt, dl), lambda b, t: (b, t, x_blk + 1)),
                  pl.BlockSpec((cw.shape[0], dl), lambda b, t: (0, 0)),
                  vec(),
                  pl.BlockSpec((nblk, LRU_BLOCK, LRU_BLOCK), lambda b, t: (0, 0, 0)),
                  vec(),
                  pl.BlockSpec((nblk, LRU_BLOCK, LRU_BLOCK), lambda b, t: (0, 0, 0)),
                  vec(), vec(), vec()],
        out_specs=pl.BlockSpec((None, bt, dl), lambda b, t: (b, t, 0)),
        scratch_shapes=[pltpu.VMEM((SUBLANES, dl), F32), pltpu.VMEM((SUBLANES, dl), F32),
                        pltpu.VMEM((bt, dl), F32), pltpu.VMEM((bt, dl), F32)],
        compiler_params=_cparams(("arbitrary", "arbitrary")),
        name="rglru",
    )(proj, proj, cw, cb, wa, ba, wx, bx, lam, g)


def _dsa_kernel(q_ref, *refs, k_sel, rep, n_qi):
    qi_refs = refs[:n_qi]
    (k_ref, v_ref, ki_ref, w_ref, g_ref, o_ref,
     key_scr, bias_scr, m_scr, l_scr, acc_scr, out_scr) = refs[n_qi:]
    bq = q_ref.shape[0]
    nkb_all, kb_sz, _ = key_scr.shape
    n_groups = q_ref.shape[1] // (rep * HEAD_DIM)
    qblk = pl.program_id(1)
    nkb = lax.div((qblk + 1) * bq + kb_sz - 1, kb_sz)
    qi_w = qi_refs[0].shape[1]
    n_idx_heads = n_qi * qi_w // IDX_DIM
    lane_tiles = kb_sz // LANES
    tile = lambda v: jnp.concatenate([v] * lane_tiles, axis=1)
    down = lambda v: jnp.concatenate([v] * (kb_sz // SUBLANES), axis=0)

    w_t = jnp.transpose(w_ref[...])
    w_off = IDX_DIM
    q_heads = []
    for h in range(n_idx_heads):
        piece, col = divmod(h * IDX_DIM, qi_w)
        q_heads.append(qi_refs[piece][:, col:col + IDX_DIM])

    def score_block(kb, carry):
        rows = pl.ds(pl.multiple_of(kb * kb_sz, kb_sz), kb_sz)
        ki = ki_ref[rows, 0:IDX_DIM].astype(BF16)
        acc = jnp.zeros((kb_sz, bq), F32)
        for h, qh in enumerate(q_heads):
            dots = lax.dot_general(ki, qh, (((1,), (1,)), ((), ())), preferred_element_type=F32)
            acc = acc + jnp.maximum(dots, 0.0) * w_t[w_off + h:w_off + h + 1, :]
        bits = lax.bitcast_convert_type(acc, I32)
        key = jnp.where(bits < 0, bits ^ jnp.int32(0x7FFFFFFF), bits)
        kpos = kb * kb_sz + lax.broadcasted_iota(I32, (kb_sz, bq), 0)
        qpos = qblk * bq + lax.broadcasted_iota(I32, (kb_sz, bq), 1)
        key_scr[kb] = jnp.where(kpos <= qpos, key, jnp.int32(INT_MIN))
        return carry

    lax.fori_loop(0, nkb, score_block, 0)

    def count_keys(hit_fn):
        lanes = COUNT_CHAINS

        def body(kb, c):
            ones = jnp.where(hit_fn(kb), 1.0, 0.0)
            return c + jnp.sum(ones.reshape(lanes, kb_sz // (lanes * SUBLANES), SUBLANES, bq), axis=1)
        c = lax.fori_loop(0, nkb, body, jnp.zeros((lanes, SUBLANES, bq), F32))
        return jnp.broadcast_to(jnp.sum(c.reshape(lanes * SUBLANES, bq), axis=0, keepdims=True), (SUBLANES, bq))

    def count_ge(cand):
        cand_rows = down(cand)
        return count_keys(lambda kb: key_scr[kb] >= cand_rows)

    def bit_step(it, carry):
        ans, n_ge = carry
        cand_u = ans | lax.shift_left(jnp.int32(1), 31 - it)
        cnt = count_ge(cand_u ^ jnp.int32(INT_MIN))
        ok = cnt >= k_sel
        return jnp.where(ok, cand_u, ans), jnp.where(ok, cnt, n_ge)

    ans, n_ge = lax.fori_loop(0, 32, bit_step,
                              (jnp.zeros((SUBLANES, bq), I32), jnp.zeros((SUBLANES, bq), F32)))
    thr = jnp.maximum(ans ^ jnp.int32(INT_MIN), jnp.int32(INT_MIN + 1))
    thr_rows = down(thr)

    @pl.when(jnp.max(n_ge) > k_sel)
    def _():
        need = k_sel - count_ge(thr + 1)
        pos_bits = (nkb_all * kb_sz - 1).bit_length()

        def ties_before(pos):
            pos_rows = down(pos)

            def hit(kb):
                kpos = kb * kb_sz + lax.broadcasted_iota(I32, (kb_sz, bq), 0)
                return (key_scr[kb] == thr_rows) & (kpos < pos_rows)
            return count_keys(hit)

        def pos_step(it, last):
            cand = last | lax.shift_left(jnp.int32(1), pos_bits - 1 - it)
            return jnp.where(ties_before(cand) < need, cand, last)

        last_rows = down(lax.fori_loop(0, pos_bits, pos_step, jnp.zeros((SUBLANES, bq), I32)))

        def demote(kb, carry):
            key = key_scr[kb]
            kpos = kb * kb_sz + lax.broadcasted_iota(I32, (kb_sz, bq), 0)
            key_scr[kb] = jnp.where((key == thr_rows) & (kpos > last_rows), key - 1, key)
            return carry

        lax.fori_loop(0, nkb, demote, 0)

    def to_bias(kb, carry):
        bias = jnp.where(key_scr[kb] >= thr_rows, 0.0, NEG_BIG).astype(F32)
        bias_scr[kb] = jnp.transpose(bias)
        return carry

    lax.fori_loop(0, nkb, to_bias, 0)

    c_exp = (HEAD_DIM ** -0.5) * LOG2_E
    sweep = m_scr.shape[0]
    for g0 in range(0, n_groups, sweep):
        heads = range(g0, g0 + sweep)
        qss = [jnp.concatenate(
            [q_ref[:, (g * rep + r) * HEAD_DIM:(g * rep + r + 1) * HEAD_DIM] for r in range(rep)], axis=0)
            for g in heads]
        m_scr[...] = jnp.full_like(m_scr, NEG_BIG)
        l_scr[...] = jnp.zeros_like(l_scr)
        acc_scr[...] = jnp.zeros_like(acc_scr)

        def kv_block(kb, carry):
            rows = pl.ds(pl.multiple_of(kb * kb_sz, kb_sz), kb_sz)
            bias = jnp.concatenate([bias_scr[kb]] * rep, axis=0)
            logits = [lax.dot_general(qs, k_ref[rows, g * HEAD_DIM:(g + 1) * HEAD_DIM],
                                      (((1,), (1,)), ((), ())), preferred_element_type=F32)
                      for qs, g in zip(qss, heads)]
            for slot, (s, g) in enumerate(zip(logits, heads)):
                s = s + bias
                m_prev = m_scr[slot]
                m_new = jnp.maximum(m_prev, jnp.max(s, axis=1, keepdims=True))
                alpha = jnp.exp2((m_prev - m_new) * c_exp)
                p = jnp.exp2((s - tile(m_new)) * c_exp)
                l_scr[slot] = alpha * l_scr[slot] + jnp.sum(p, axis=1, keepdims=True)
                acc_scr[slot] = alpha * acc_scr[slot] + jnp.dot(
                    p.astype(BF16), v_ref[rows, g * HEAD_DIM:(g + 1) * HEAD_DIM], preferred_element_type=F32)
                m_scr[slot] = m_new
            return carry

        lax.fori_loop(0, nkb, kv_block, 0)
        for slot, g in enumerate(heads):
            o = acc_scr[slot] / l_scr[slot]
            for r in range(rep):
                out_scr[:, (g * rep + r) * HEAD_DIM:(g * rep + r + 1) * HEAD_DIM] = o[r * bq:(r + 1) * bq, :]

    y = out_scr[...]
    o_ref[...] = (y * _rms_scale(y) * g_ref[...]).astype(BF16)


def _dsa(proj, q_off, qi_off, k_off, small, g, d_attn, d_kv, bq, kb_sz):
    bsz, s, _ = proj.shape
    nkb_all = s // kb_sz
    rep = d_attn // d_kv
    k_sel = min(TOPK_MAX, s // 4)
    d_qi = N_IDX_HEADS * IDX_DIM
    qi_w = math.gcd(qi_off, d_qi)
    n_qi = d_qi // qi_w
    q_blk = q_off // d_attn
    k_blk = k_off // d_kv
    kern = functools.partial(_dsa_kernel, k_sel=k_sel, rep=rep, n_qi=n_qi)
    qi_specs = [pl.BlockSpec((None, bq, qi_w), lambda b, i, t=t: (b, i, qi_off // qi_w + t)) for t in range(n_qi)]
    return pl.pallas_call(
        kern,
        out_shape=jax.ShapeDtypeStruct((bsz, s, d_attn), BF16),
        grid=(bsz, s // bq),
        in_specs=[pl.BlockSpec((None, bq, d_attn), lambda b, i: (b, i, q_blk))] + qi_specs + [
                  pl.BlockSpec((None, s, d_kv), lambda b, i: (b, 0, k_blk)),
                  pl.BlockSpec((None, s, d_kv), lambda b, i: (b, 0, k_blk + 1)),
                  pl.BlockSpec((None, s, LANES), lambda b, i: (b, 0, 0)),
                  pl.BlockSpec((None, bq, LANES), lambda b, i: (b, i, 0)),
                  pl.BlockSpec((1, d_attn), lambda b, i: (0, 0))],
        out_specs=pl.BlockSpec((None, bq, d_attn), lambda b, i: (b, i, 0)),
        scratch_shapes=[pltpu.VMEM((nkb_all, kb_sz, bq), I32),
                        pltpu.VMEM((nkb_all, bq, kb_sz), F32),
                        pltpu.VMEM((ATTN_HEADS_PER_SWEEP, rep * bq, LANES), F32),
                        pltpu.VMEM((ATTN_HEADS_PER_SWEEP, rep * bq, LANES), F32),
                        pltpu.VMEM((ATTN_HEADS_PER_SWEEP, rep * bq, HEAD_DIM), F32),
                        pltpu.VMEM((bq, d_attn), F32)],
        compiler_params=_cparams(("arbitrary", "arbitrary")),
        name="dsa",
    )(proj, *([proj] * n_qi), proj, proj, small, small, g)


def _outproj_kernel(a1_ref, a2_ref, w_ref, x_ref, gpost_ref, gt_ref, gpre_ref, sc_ref, sh_ref,
                    x1_ref, h2_ref, mix_ref):
    j = pl.program_id(2)
    n_tiles, bm, tn = mix_ref.shape
    k1 = a1_ref.shape[1]
    mix_ref[j] = (jnp.dot(a1_ref[...], w_ref[0:k1, :], preferred_element_type=F32)
                  + jnp.dot(a2_ref[...], w_ref[k1:, :], preferred_element_type=F32))

    @pl.when(j == n_tiles - 1)
    def _():
        post = gt_ref[...] * gpost_ref[...]
        pre = gpre_ref[...] * (1.0 + sc_ref[...])
        sh = sh_ref[...]

        def rows_out(r, carry):
            rows = pl.ds(pl.multiple_of(r * SUBLANES, SUBLANES), SUBLANES)
            m = jnp.concatenate([mix_ref[t, rows, :] for t in range(n_tiles)], axis=1)
            x1 = x_ref[rows, :] + post * (m * _rms_scale(m))
            x1_ref[rows, :] = x1
            h2_ref[rows, :] = ((x1 * _rms_scale(x1)) * pre + sh).astype(BF16)
            return carry

        lax.fori_loop(0, bm // SUBLANES, rows_out, 0, unroll=8)


def _out_proj(a1, a2, w, x, gpost, gt, gpre, sc, sh, bm, tn):
    bsz, s, d = x.shape
    k1, k2 = a1.shape[2], a2.shape[2]
    n_tiles = d // tn
    vec = lambda: pl.BlockSpec((1, d), lambda b, i, j: (0, 0))
    bvec = lambda: pl.BlockSpec((None, 1, d), lambda b, i, j: (b, 0, 0))
    big = lambda bufs=None: pl.BlockSpec((None, bm, d), lambda b, i, j: (b, i, 0), pipeline_mode=bufs)
    return pl.pallas_call(
        _outproj_kernel,
        out_shape=(jax.ShapeDtypeStruct((bsz, s, d), F32), jax.ShapeDtypeStruct((bsz, s, d), BF16)),
        grid=(bsz, s // bm, n_tiles),
        in_specs=[pl.BlockSpec((None, bm, k1), lambda b, i, j: (b, i, 0), pipeline_mode=pl.Buffered(1)),
                  pl.BlockSpec((None, bm, k2), lambda b, i, j: (b, i, 0), pipeline_mode=pl.Buffered(1)),
                  pl.BlockSpec((k1 + k2, tn), lambda b, i, j: (0, j)),
                  big(), vec(), bvec(), vec(), bvec(), bvec()],
        out_specs=(big(), big(pl.Buffered(1))),
        scratch_shapes=[pltpu.VMEM((n_tiles, bm, tn), F32)],
        compiler_params=_cparams(("arbitrary", "arbitrary", "arbitrary")),
        name="out_proj",
    )(a1, a2, w, x, gpost, gt, gpre, sc, sh)


FFN_ROW_CHUNK = 64
FFN_DOWN_PIECES = 8


def _ffn_kernel(h_ref, wg_ref, wu_ref, cwb_ref, wo_ref, x_ref, g_ref, gt_ref,
                o_ref, u_even, u_odd, act_even, act_odd, carry_ref):
    i = pl.program_id(1)
    s = pl.program_id(2)
    nj = pl.num_programs(2) - 2
    bm, d = h_ref.shape
    tf = wo_ref.shape[0]
    tile = jnp.minimum(s, nj - 1)
    u_bufs = (u_even, u_odd)
    act_bufs = (act_even, act_odd)

    @pl.when(s == 0)
    def _():
        o_ref[...] = jnp.zeros_like(o_ref)

    @pl.when(jnp.logical_and(i == 0, s < nj))
    def _():
        carry_ref[tile] = jnp.zeros(carry_ref.shape[1:], F32)

    def step(p, up_proj, finish, down_proj):
        u_new, u_last = u_bufs[p], u_bufs[1 - p]
        act_new, act_last = act_bufs[1 - p], act_bufs[p]
        cwb = cwb_ref[...]
        kw = cwb.shape[0] - 1
        cwg, cwu, cbg, cbu = cwb[0:kw, 0:tf], cwb[0:kw, tf:], cwb[kw:, 0:tf], cwb[kw:, tf:]

        def conv(r0, c0, cw, cb):
            cols = slice(c0, c0 + tf)
            return (cb + cw[2:3, :] * u_last[r0:r0 + FFN_ROW_CHUNK, cols]
                    + cw[1:2, :] * u_last[r0 - 1:r0 - 1 + FFN_ROW_CHUNK, cols]
                    + cw[0:1, :] * u_last[r0 - 2:r0 - 2 + FFN_ROW_CHUNK, cols])

        def finish_rows(c, after):
            if after is None:
                zero = 0.0
            else:
                bits = lax.bitcast_convert_type(after[0:1, 0:tf], jnp.uint32)
                zero = lax.shift_right_logical(lax.shift_right_logical(bits, jnp.uint32(16)),
                                               jnp.uint32(16)).astype(F32)
            r0 = SUBLANES + c * FFN_ROW_CHUNK
            gate = conv(r0, 0, cwg, cbg + zero)
            up = conv(r0, tf, cwu, cbu + zero)
            act_new[c * FFN_ROW_CHUNK:(c + 1) * FFN_ROW_CHUNK, :] = (_gelu_tanh(gate) * up).astype(BF16)

        n_chunks = bm // FFN_ROW_CHUNK
        n_pieces = min(FFN_DOWN_PIECES, n_chunks)
        per_piece = n_chunks // n_pieces
        assert per_piece * n_pieces == n_chunks
        nw = d // n_pieces
        act_l = act_last[...] if down_proj else None
        for pc in range(n_pieces):
            piece = None
            if down_proj:
                cols = slice(pc * nw, (pc + 1) * nw)
                piece = jnp.dot(act_l, wo_ref[:, cols], preferred_element_type=F32)
                o_ref[:, cols] += piece
            if finish:
                for c in range(pc * per_piece, (pc + 1) * per_piece):
                    finish_rows(c, piece)

        if up_proj:
            w_gu = jnp.concatenate([wg_ref[...], wu_ref[...]], axis=1)
            u = jnp.dot(h_ref[...], w_gu, preferred_element_type=F32)
            u_new[0:SUBLANES, :] = carry_ref[tile]
            u_new[SUBLANES:, :] = u
            carry_ref[tile] = u[bm - SUBLANES:, :]

    parity = lax.rem(s, 2)
    steady = jnp.logical_and(s >= 2, s < nj)
    variants = [(s == 0, 0, (True, False, False)),
                (s == 1, 1, (True, True, False)),
                (jnp.logical_and(steady, parity == 0), 0, (True, True, True)),
                (jnp.logical_and(steady, parity == 1), 1, (True, True, True)),
                (s == nj, nj % 2, (False, True, True)),
                (s == nj + 1, (nj + 1) % 2, (False, False, True))]
    for cond, p, stages in variants:
        pl.when(cond)(functools.partial(step, p, *stages))

    @pl.when(s == nj + 1)
    def _():
        scale = gt_ref[...] * g_ref[...]
        for r0 in range(0, bm, SUBLANES):
            f = o_ref[r0:r0 + SUBLANES, :]
            o_ref[r0:r0 + SUBLANES, :] = x_ref[r0:r0 + SUBLANES, :] + scale * (f * _rms_scale(f))


def _conv_ffn(h, w_in, cw, cb, wo, x, g, gt, bm, tf):
    bsz, s, d = x.shape
    nj = wo.shape[0] // tf
    kw = cw.shape[0]
    cwb = jnp.concatenate([cw, cb], axis=0).reshape(kw + 1, 2, nj, tf)
    cwb = jnp.transpose(cwb, (2, 0, 1, 3)).reshape(nj, kw + 1, 2 * tf)
    up_t = lambda j: jnp.minimum(j, nj - 1)
    ew_t = lambda j: jnp.clip(j - 1, 0, nj - 1)
    dn_t = lambda j: jnp.clip(j - 2, 0, nj - 1)
    big = lambda: pl.BlockSpec((None, bm, d), lambda b, i, j: (b, i, 0))
    return pl.pallas_call(
        _ffn_kernel,
        out_shape=jax.ShapeDtypeStruct((bsz, s, d), F32),
        grid=(bsz, s // bm, nj + 2),
        in_specs=[pl.BlockSpec((None, bm, d), lambda b, i, j: (b, i, 0), pipeline_mode=pl.Buffered(1)),
                  pl.BlockSpec((d, tf), lambda b, i, j: (0, up_t(j))),
                  pl.BlockSpec((d, tf), lambda b, i, j: (0, nj + up_t(j))),
                  pl.BlockSpec((None, kw + 1, 2 * tf), lambda b, i, j: (ew_t(j), 0, 0)),
                  pl.BlockSpec((tf, d), lambda b, i, j: (dn_t(j), 0)),
                  big(),
                  pl.BlockSpec((1, d), lambda b, i, j: (0, 0)),
                  pl.BlockSpec((None, 1, d), lambda b, i, j: (b, 0, 0))],
        out_specs=big(),
        scratch_shapes=[pltpu.VMEM((SUBLANES + bm, 2 * tf), F32),
                        pltpu.VMEM((SUBLANES + bm, 2 * tf), F32),
                        pltpu.VMEM((bm, tf), BF16),
                        pltpu.VMEM((bm, tf), BF16),
                        pltpu.VMEM((nj, SUBLANES, 2 * tf), F32)],
        compiler_params=_cparams(("arbitrary", "arbitrary", "arbitrary")),
        name="conv_ffn",
    )(h, w_in, w_in, cwb, wo, x, g, gt)


def _layer(x, c, w_ada, b_ada, g_pre_mix, g_post_mix, g_pre_ffn, g_post_ffn, w_in,
           conv_lru_w, conv_lru_b, w_rg_a, b_rg_a, w_rg_x, b_rg_x, lru_lambda,
           g_grp_lru, g_grp_attn, w_out, w_ffn_in, conv_ffn_w, conv_ffn_b, w_ffn_out,
           *, bm_in, bn_in, bt_lru, bq, kb_sz, bm_out, tn_out, bm_ffn, tf):
    bsz, s, d = x.shape
    d_lru = w_rg_a.shape[0] * LRU_BLOCK
    d_attn = g_grp_attn.shape[0]
    d_qi = N_IDX_HEADS * IDX_DIM
    d_kv = (w_in.shape[1] - 2 * d_lru - d_attn - d_qi - IDX_DIM - N_IDX_HEADS) // 2
    row = lambda v: v.reshape(1, -1)

    mod = _modulation(c, w_ada, b_ada)
    sh1, sc1, gt1, sh2, sc2, gt2 = [m.reshape(bsz, 1, d) for m in jnp.split(mod, 6, axis=-1)]

    q_off = 2 * d_lru
    k_off = q_off + d_attn
    qi_off = k_off + 2 * d_kv
    n_main = qi_off + d_qi
    n_small = w_in.shape[1] - n_main
    w_in_b = w_in.astype(BF16)
    w_small = jnp.concatenate([w_in[:, n_main:], jnp.zeros((d, LANES - n_small), F32)], axis=1).astype(BF16)

    proj, small = _in_proj(x, row(g_pre_mix), sc1, sh1, w_in_b, n_main, w_small, bm_in, bn_in)

    y_lru = _rglru(proj, 0, conv_lru_w, row(conv_lru_b), w_rg_a.astype(BF16), row(b_rg_a),
                   w_rg_x.astype(BF16), row(b_rg_x), row(lru_lambda), row(g_grp_lru), bt_lru)

    y_attn = _dsa(proj, q_off, qi_off, k_off, small, row(g_grp_attn), d_attn, d_kv, bq, kb_sz)

    x1, h2 = _out_proj(y_lru, y_attn, w_out.astype(BF16), x, row(g_post_mix), gt1, row(g_pre_ffn), sc2, sh2,
                       bm_out, tn_out)

    d_ff = w_ffn_out.shape[0]
    pad = -d_ff % tf
    halves = lambda a: jnp.concatenate([jnp.pad(a[:, :d_ff], ((0, 0), (0, pad))),
                                        jnp.pad(a[:, d_ff:], ((0, 0), (0, pad)))], axis=1)
    return _conv_ffn(h2, halves(w_ffn_in).astype(BF16), halves(conv_ffn_w), halves(row(conv_ffn_b)),
                     jnp.pad(w_ffn_out, ((0, pad), (0, 0))).astype(BF16), x1, row(g_post_ffn), gt2, bm_ffn, tf)


_TILES = dict(bm_in=512, bn_in=1024, bt_lru=256, bq=256, kb_sz=512, bm_out=512, tn_out=512, bm_ffn=512, tf=256)


def kernel(x, c, w_ada, b_ada, g_pre_mix, g_post_mix, g_pre_ffn, g_post_ffn, w_in, conv_lru_w, conv_lru_b,
           w_rg_a, b_rg_a, w_rg_x, b_rg_x, lru_lambda, g_grp_lru, g_grp_attn, w_out, w_ffn_in,
           conv_ffn_w, conv_ffn_b, w_ffn_out):
    for l in range(w_ada.shape[0]):
        x = _layer(x, c, w_ada[l], b_ada[l], g_pre_mix[l], g_post_mix[l], g_pre_ffn[l], g_post_ffn[l],
                   w_in[l], conv_lru_w[l], conv_lru_b[l], w_rg_a[l], b_rg_a[l], w_rg_x[l], b_rg_x[l],
                   lru_lambda[l], g_grp_lru[l], g_grp_attn[l], w_out[l], w_ffn_in[l], conv_ffn_w[l],
                   conv_ffn_b[l], w_ffn_out[l], **_TILES)
    return x
```

```python
import functools
import math

import jax
import jax.numpy as jnp
from jax import lax
from jax.experimental import pallas as pl
from jax.experimental.pallas import tpu as pltpu

F32 = jnp.float32
BF16 = jnp.bfloat16
I32 = jnp.int32

RMS_EPS = 1e-6
LRU_C = 8.0
LRU_BLOCK = 128
HEAD_DIM = 128
N_KV_HEADS = 4
N_IDX_HEADS = 32
IDX_DIM = 64
TOPK_MAX = 256

LANES = 128
SUBLANES = 8
VMEM_LIMIT = 59 * 1024 * 1024

ATTN_HEADS_PER_SWEEP = 2
COUNT_CHAINS = 2
INT_MIN = -(2 ** 31)
NEG_BIG = -1e30
LOG2_E = 1.4426950408889634


def _cparams(sem):
    return pltpu.CompilerParams(dimension_semantics=sem, vmem_limit_bytes=VMEM_LIMIT)


def _rms_scale(v):
    return lax.rsqrt(jnp.mean(v * v, axis=-1, keepdims=True) + RMS_EPS)


def _gelu_tanh(v):
    inner = v * (1.0 + 0.044715 * (v * v))
    return v / (1.0 + jnp.exp((-2.0 * 0.7978845608028654) * inner))


def _shift_rows(cur, prev8, d):
    rolled = pltpu.roll(cur, d, 0)
    head = pltpu.roll(prev8, d, 0)
    row = lax.broadcasted_iota(I32, (SUBLANES, cur.shape[1]), 0)
    top = jnp.where(row < d, head, rolled[0:SUBLANES])
    return jnp.concatenate([top, rolled[SUBLANES:]], axis=0)


def _mod_kernel(c_ref, w_ref, b_ref, o_ref):
    c = c_ref[...]
    s = c * jax.nn.sigmoid(c)
    w = w_ref[...].astype(BF16)
    s_hi = s.astype(BF16)
    s_lo = (s - s_hi.astype(F32)).astype(BF16)
    o_ref[...] = (jnp.dot(s_hi, w, preferred_element_type=F32)
                  + jnp.dot(s_lo, w, preferred_element_type=F32) + b_ref[...])


def _modulation(c, w_ada, b_ada, tn=512):
    bsz, d = c.shape
    n = w_ada.shape[1]
    rows = -(-bsz // SUBLANES) * SUBLANES
    c_pad = jnp.zeros((rows, d), F32).at[:bsz].set(c)
    out = pl.pallas_call(
        _mod_kernel,
        out_shape=jax.ShapeDtypeStruct((rows, n), F32),
        grid=(n // tn,),
        in_specs=[pl.BlockSpec((rows, d), lambda j: (0, 0)),
                  pl.BlockSpec((d, tn), lambda j: (0, j)),
                  pl.BlockSpec((1, tn), lambda j: (0, j))],
        out_specs=pl.BlockSpec((rows, tn), lambda j: (0, j)),
        compiler_params=_cparams(("arbitrary",)),
        name="modulation",
    )(c_pad, w_ada, b_ada.reshape(1, n))
    return out[:bsz]


def _inproj_kernel(x_ref, g_ref, sc_ref, sh_ref, w_ref, ws_ref, o_ref, os_ref, h_ref):
    @pl.when(pl.program_id(2) == 0)
    def _():
        x = x_ref[...]
        y = x * _rms_scale(x) * g_ref[...]
        h = (y * (1.0 + sc_ref[...]) + sh_ref[...]).astype(BF16)
        h_ref[...] = h
        os_ref[...] = jnp.dot(h, ws_ref[...], preferred_element_type=F32)

    o_ref[...] = jnp.dot(h_ref[...], w_ref[...], preferred_element_type=F32).astype(BF16)


def _in_proj(x, g, sc, sh, w_main, n, w_small, bm, bn):
    bsz, s, d = x.shape
    ns = w_small.shape[1]
    return pl.pallas_call(
        _inproj_kernel,
        out_shape=(jax.ShapeDtypeStruct((bsz, s, n), BF16), jax.ShapeDtypeStruct((bsz, s, ns), F32)),
        grid=(bsz, s // bm, n // bn),
        in_specs=[pl.BlockSpec((None, bm, d), lambda b, i, j: (b, i, 0)),
                  pl.BlockSpec((1, d), lambda b, i, j: (0, 0)),
                  pl.BlockSpec((None, 1, d), lambda b, i, j: (b, 0, 0)),
                  pl.BlockSpec((None, 1, d), lambda b, i, j: (b, 0, 0)),
                  pl.BlockSpec((d, bn), lambda b, i, j: (0, j)),
                  pl.BlockSpec((d, ns), lambda b, i, j: (0, 0))],
        out_specs=(pl.BlockSpec((None, bm, bn), lambda b, i, j: (b, i, j)),
                   pl.BlockSpec((None, bm, ns), lambda b, i, j: (b, i, 0))),
        scratch_shapes=[pltpu.VMEM((bm, d), BF16)],
        compiler_params=_cparams(("arbitrary", "arbitrary", "arbitrary")),
        name="in_proj",
    )(x, g, sc, sh, w_main, w_small)


def _rglru_kernel(xl_ref, gate_ref, cw_ref, cb_ref, wa_ref, ba_ref, wx_ref, bx_ref, lam_ref, g_ref,
                  o_ref, halo_ref, hc_ref, a_scr, b_scr):
    bt, dl = xl_ref.shape
    nblk = dl // LRU_BLOCK

    @pl.when(pl.program_id(1) == 0)
    def _():
        halo_ref[...] = jnp.zeros_like(halo_ref)
        hc_ref[...] = jnp.zeros_like(hc_ref)

    xl = xl_ref[...].astype(F32)
    prev = halo_ref[...]
    halo_ref[...] = xl[bt - SUBLANES:, :]

    cw = cw_ref[...]
    kw = cw.shape[0]
    xc = cb_ref[...] + cw[kw - 1:kw, :] * xl
    for dlt in range(1, kw):
        xc = xc + cw[kw - 1 - dlt:kw - dlt, :] * _shift_rows(xl, prev, dlt)

    xcb = xc.astype(BF16)
    r_parts, i_parts = [], []
    for n in range(nblk):
        blk = xcb[:, n * LRU_BLOCK:(n + 1) * LRU_BLOCK]
        r_parts.append(jnp.dot(blk, wa_ref[n], preferred_element_type=F32))
        i_parts.append(jnp.dot(blk, wx_ref[n], preferred_element_type=F32))
    r = jax.nn.sigmoid(jnp.concatenate(r_parts, axis=1) + ba_ref[...])
    ig = jax.nn.sigmoid(jnp.concatenate(i_parts, axis=1) + bx_ref[...])

    z = -lam_ref[...]
    softplus = jnp.maximum(z, 0.0) + jnp.log(1.0 + jnp.exp(-jnp.abs(z)))
    log_a = (-LRU_C * softplus) * r
    a = jnp.exp(log_a)
    u = jnp.sqrt(1.0 - jnp.exp(2.0 * log_a)) * (ig * xc)

    rowmod = lax.broadcasted_iota(I32, (bt, dl), 0) % SUBLANES
    b = u
    for sft in (1, 2, 4):
        keep = rowmod >= sft
        a_prev = jnp.where(keep, pltpu.roll(a, sft, 0), 1.0)
        b_prev = jnp.where(keep, pltpu.roll(b, sft, 0), 0.0)
        b = a * b_prev + b
        a = a * a_prev
    a_scr[...] = a
    b_scr[...] = b

    def group(gi, carry):
        rows = pl.ds(pl.multiple_of(gi * SUBLANES, SUBLANES), SUBLANES)
        h = a_scr[rows, :] * carry + b_scr[rows, :]
        b_scr[rows, :] = h
        return jnp.broadcast_to(h[SUBLANES - 1:SUBLANES, :], (SUBLANES, dl))

    hc_ref[...] = lax.fori_loop(0, bt // SUBLANES, group, hc_ref[...])

    y = b_scr[...] * _gelu_tanh(gate_ref[...].astype(F32))
    o_ref[...] = (y * _rms_scale(y) * g_ref[...]).astype(BF16)


def _rglru(proj, x_off, cw, cb, wa, ba, wx, bx, lam, g, bt):
    bsz, s, _ = proj.shape
    dl = cw.shape[1]
    nblk = dl // LRU_BLOCK
    x_blk = x_off // dl
    vec = lambda: pl.BlockSpec((1, dl), lambda b, t: (0, 0))
    return pl.pallas_call(
        _rglru_kernel,
        out_shape=jax.ShapeDtypeStruct((bsz, s, dl), BF16),
        grid=(bsz, s // bt),
        in_specs=[pl.BlockSpec((None, bt, dl), lambda b, t: (b, t, x_blk)),
                  pl.BlockSpec((None, bt, dl), lambda b, t: (b, t, x_blk + 1)),
                  pl.BlockSpec((cw.shape[0], dl), lambda b, t: (0, 0)),
                  vec(),
                  pl.BlockSpec((nblk, LRU_BLOCK, LRU_BLOCK), lambda b, t: (0, 0, 0)),
                  vec(),
                  pl.BlockSpec((nblk, LRU_BLOCK, LRU_BLOCK), lambda b, t: (0, 0, 0)),
                  vec(), vec(), vec()],
        out_specs=pl.BlockSpec((None, bt, dl), lambda b, t: (b, t, 0)),
        scratch_shapes=[pltpu.VMEM((SUBLANES, dl), F32), pltpu.VMEM((SUBLANES, dl), F32),
                        pltpu.VMEM((bt, dl), F32), pltpu.VMEM((bt, dl), F32)],
        compiler_params=_cparams(("arbitrary", "arbitrary")),
        name="rglru",
    )(proj, proj, cw, cb, wa, ba, wx, bx, lam, g)


def _dsa_kernel(q_ref, *refs, k_sel, rep, n_qi):
    qi_refs = refs[:n_qi]
    (k_ref, v_ref, ki_ref, w_ref, g_ref, o_ref,
     key_scr, bias_scr, m_scr, l_scr, acc_scr, out_scr) = refs[n_qi:]
    bq = q_ref.shape[0]
    nkb_all, kb_sz, _ = key_scr.shape
    n_groups = q_ref.shape[1] // (rep * HEAD_DIM)
    qblk = pl.program_id(1)
    nkb = lax.div((qblk + 1) * bq + kb_sz - 1, kb_sz)
    qi_w = qi_refs[0].shape[1]
    n_idx_heads = n_qi * qi_w // IDX_DIM
    lane_tiles = kb_sz // LANES
    tile = lambda v: jnp.concatenate([v] * lane_tiles, axis=1)
    down = lambda v: jnp.concatenate([v] * (kb_sz // SUBLANES), axis=0)

    w_t = jnp.transpose(w_ref[...])
    w_off = IDX_DIM
    q_heads = []
    for h in range(n_idx_heads):
        piece, col = divmod(h * IDX_DIM, qi_w)
        q_heads.append(qi_refs[piece][:, col:col + IDX_DIM])

    def score_block(kb, carry):
        rows = pl.ds(pl.multiple_of(kb * kb_sz, kb_sz), kb_sz)
        ki = ki_ref[rows, 0:IDX_DIM].astype(BF16)
        acc = jnp.zeros((kb_sz, bq), F32)
        for h, qh in enumerate(q_heads):
            dots = lax.dot_general(ki, qh, (((1,), (1,)), ((), ())), preferred_element_type=F32)
            acc = acc + jnp.maximum(dots, 0.0) * w_t[w_off + h:w_off + h + 1, :]
        bits = lax.bitcast_convert_type(acc, I32)
        key = jnp.where(bits < 0, bits ^ jnp.int32(0x7FFFFFFF), bits)
        kpos = kb * kb_sz + lax.broadcasted_iota(I32, (kb_sz, bq), 0)
        qpos = qblk * bq + lax.broadcasted_iota(I32, (kb_sz, bq), 1)
        key_scr[kb] = jnp.where(kpos <= qpos, key, jnp.int32(INT_MIN))
        return carry

    lax.fori_loop(0, nkb, score_block, 0)

    def count_keys(hit_fn):
        lanes = COUNT_CHAINS

        def body(kb, c):
            ones = jnp.where(hit_fn(kb), 1.0, 0.0)
            return c + jnp.sum(ones.reshape(lanes, kb_sz // (lanes * SUBLANES), SUBLANES, bq), axis=1)
        c = lax.fori_loop(0, nkb, body, jnp.zeros((lanes, SUBLANES, bq), F32))
        return jnp.broadcast_to(jnp.sum(c.reshape(lanes * SUBLANES, bq), axis=0, keepdims=True), (SUBLANES, bq))

    def count_ge(cand):
        cand_rows = down(cand)
        return count_keys(lambda kb: key_scr[kb] >= cand_rows)

    def bit_step(it, carry):
        ans, n_ge = carry
        cand_u = ans | lax.shift_left(jnp.int32(1), 31 - it)
        cnt = count_ge(cand_u ^ jnp.int32(INT_MIN))
        ok = cnt >= k_sel
        return jnp.where(ok, cand_u, ans), jnp.where(ok, cnt, n_ge)

    ans, n_ge = lax.fori_loop(0, 32, bit_step,
                              (jnp.zeros((SUBLANES, bq), I32), jnp.zeros((SUBLANES, bq), F32)))
    thr = jnp.maximum(ans ^ jnp.int32(INT_MIN), jnp.int32(INT_MIN + 1))
    thr_rows = down(thr)

    @pl.when(jnp.max(n_ge) > k_sel)
    def _():
        need = k_sel - count_ge(thr + 1)
        pos_bits = (nkb_all * kb_sz - 1).bit_length()

        def ties_before(pos):
            pos_rows = down(pos)

            def hit(kb):
                kpos = kb * kb_sz + lax.broadcasted_iota(I32, (kb_sz, bq), 0)
                return (key_scr[kb] == thr_rows) & (kpos < pos_rows)
            return count_keys(hit)

        def pos_step(it, last):
            cand = last | lax.shift_left(jnp.int32(1), pos_bits - 1 - it)
            return jnp.where(ties_before(cand) < need, cand, last)

        last_rows = down(lax.fori_loop(0, pos_bits, pos_step, jnp.zeros((SUBLANES, bq), I32)))

        def demote(kb, carry):
            key = key_scr[kb]
            kpos = kb * kb_sz + lax.broadcasted_iota(I32, (kb_sz, bq), 0)
            key_scr[kb] = jnp.where((key == thr_rows) & (kpos > last_rows), key - 1, key)
            return carry

        lax.fori_loop(0, nkb, demote, 0)

    def to_bias(kb, carry):
        bias = jnp.where(key_scr[kb] >= thr_rows, 0.0, NEG_BIG).astype(F32)
        bias_scr[kb] = jnp.transpose(bias)
        return carry

    lax.fori_loop(0, nkb, to_bias, 0)

    c_exp = (HEAD_DIM ** -0.5) * LOG2_E
    sweep = m_scr.shape[0]
    for g0 in range(0, n_groups, sweep):
        heads = range(g0, g0 + sweep)
        qss = [jnp.concatenate(
            [q_ref[:, (g * rep + r) * HEAD_DIM:(g * rep + r + 1) * HEAD_DIM] for r in range(rep)], axis=0)
            for g in heads]
        m_scr[...] = jnp.full_like(m_scr, NEG_BIG)
        l_scr[...] = jnp.zeros_like(l_scr)
        acc_scr[...] = jnp.zeros_like(acc_scr)

        def kv_block(kb, carry):
            rows = pl.ds(pl.multiple_of(kb * kb_sz, kb_sz), kb_sz)
            bias = jnp.concatenate([bias_scr[kb]] * rep, axis=0)
            logits = [lax.dot_general(qs, k_ref[rows, g * HEAD_DIM:(g + 1) * HEAD_DIM],
                                      (((1,), (1,)), ((), ())), preferred_element_type=F32)
                      for qs, g in zip(qss, heads)]
            for slot, (s, g) in enumerate(zip(logits, heads)):
                s = s + bias
                m_prev = m_scr[slot]
                m_new = jnp.maximum(m_prev, jnp.max(s, axis=1, keepdims=True))
                alpha = jnp.exp2((m_prev - m_new) * c_exp)
                p = jnp.exp2((s - tile(m_new)) * c_exp)
                l_scr[slot] = alpha * l_scr[slot] + jnp.sum(p, axis=1, keepdims=True)
                acc_scr[slot] = alpha * acc_scr[slot] + jnp.dot(
                    p.astype(BF16), v_ref[rows, g * HEAD_DIM:(g + 1) * HEAD_DIM], preferred_element_type=F32)
                m_scr[slot] = m_new
            return carry

        lax.fori_loop(0, nkb, kv_block, 0)
        for slot, g in enumerate(heads):
            o = acc_scr[slot] / l_scr[slot]
            for r in range(rep):
                out_scr[:, (g * rep + r) * HEAD_DIM:(g * rep + r + 1) * HEAD_DIM] = o[r * bq:(r + 1) * bq, :]

    y = out_scr[...]
    o_ref[...] = (y * _rms_scale(y) * g_ref[...]).astype(BF16)


def _dsa(proj, q_off, qi_off, k_off, small, g, d_attn, d_kv, bq, kb_sz):
    bsz, s, _ = proj.shape
    nkb_all = s // kb_sz
    rep = d_attn // d_kv
    k_sel = min(TOPK_MAX, s // 4)
    d_qi = N_IDX_HEADS * IDX_DIM
    qi_w = math.gcd(qi_off, d_qi)
    n_qi = d_qi // qi_w
    q_blk = q_off // d_attn
    k_blk = k_off // d_kv
    kern = functools.partial(_dsa_kernel, k_sel=k_sel, rep=rep, n_qi=n_qi)
    qi_specs = [pl.BlockSpec((None, bq, qi_w), lambda b, i, t=t: (b, i, qi_off // qi_w + t)) for t in range(n_qi)]
    return pl.pallas_call(
        kern,
        out_shape=jax.ShapeDtypeStruct((bsz, s, d_attn), BF16),
        grid=(bsz, s // bq),
        in_specs=[pl.BlockSpec((None, bq, d_attn), lambda b, i: (b, i, q_blk))] + qi_specs + [
                  pl.BlockSpec((None, s, d_kv), lambda b, i: (b, 0, k_blk)),
                  pl.BlockSpec((None, s, d_kv), lambda b, i: (b, 0, k_blk + 1)),
                  pl.BlockSpec((None, s, LANES), lambda b, i: (b, 0, 0)),
                  pl.BlockSpec((None, bq, LANES), lambda b, i: (b, i, 0)),
                  pl.BlockSpec((1, d_attn), lambda b, i: (0, 0))],
        out_specs=pl.BlockSpec((None, bq, d_attn), lambda b, i: (b, i, 0)),
        scratch_shapes=[pltpu.VMEM((nkb_all, kb_sz, bq), I32),
                        pltpu.VMEM((nkb_all, bq, kb_sz), F32),
                        pltpu.VMEM((ATTN_HEADS_PER_SWEEP, rep * bq, LANES), F32),
                        pltpu.VMEM((ATTN_HEADS_PER_SWEEP, rep * bq, LANES), F32),
                        pltpu.VMEM((ATTN_HEADS_PER_SWEEP, rep * bq, HEAD_DIM), F32),
                        pltpu.VMEM((bq, d_attn), F32)],
        compiler_params=_cparams(("arbitrary", "arbitrary")),
        name="dsa",
    )(proj, *([proj] * n_qi), proj, proj, small, small, g)


def _outproj_kernel(a1_ref, a2_ref, w_ref, x_ref, gpost_ref, gt_ref, gpre_ref, sc_ref, sh_ref,
                    x1_ref, h2_ref, mix_ref):
    j = pl.program_id(2)
    n_tiles, bm, tn = mix_ref.shape
    k1 = a1_ref.shape[1]
    mix_ref[j] = (jnp.dot(a1_ref[...], w_ref[0:k1, :], preferred_element_type=F32)
                  + jnp.dot(a2_ref[...], w_ref[k1:, :], preferred_element_type=F32))

    @pl.when(j == n_tiles - 1)
    def _():
        post = gt_ref[...] * gpost_ref[...]
        pre = gpre_ref[...] * (1.0 + sc_ref[...])
        sh = sh_ref[...]

        for r0 in range(0, bm, SUBLANES):
            rows = slice(r0, r0 + SUBLANES)
            m = jnp.concatenate([mix_ref[t, rows, :] for t in range(n_tiles)], axis=1)
            x1 = x_ref[rows, :] + post * (m * _rms_scale(m))
            x1_ref[rows, :] = x1
            h2_ref[rows, :] = ((x1 * _rms_scale(x1)) * pre + sh).astype(BF16)


def _out_proj(a1, a2, w, x, gpost, gt, gpre, sc, sh, bm, tn):
    bsz, s, d = x.shape
    k1, k2 = a1.shape[2], a2.shape[2]
    n_tiles = d // tn
    vec = lambda: pl.BlockSpec((1, d), lambda b, i, j: (0, 0))
    bvec = lambda: pl.BlockSpec((None, 1, d), lambda b, i, j: (b, 0, 0))
    big = lambda bufs=None: pl.BlockSpec((None, bm, d), lambda b, i, j: (b, i, 0), pipeline_mode=bufs)
    return pl.pallas_call(
        _outproj_kernel,
        out_shape=(jax.ShapeDtypeStruct((bsz, s, d), F32), jax.ShapeDtypeStruct((bsz, s, d), BF16)),
        grid=(bsz, s // bm, n_tiles),
        in_specs=[pl.BlockSpec((None, bm, k1), lambda b, i, j: (b, i, 0), pipeline_mode=pl.Buffered(1)),
                  pl.BlockSpec((None, bm, k2), lambda b, i, j: (b, i, 0), pipeline_mode=pl.Buffered(1)),
                  pl.BlockSpec((k1 + k2, tn), lambda b, i, j: (0, j)),
                  big(), vec(), bvec(), vec(), bvec(), bvec()],
        out_specs=(big(), big(pl.Buffered(1))),
        scratch_shapes=[pltpu.VMEM((n_tiles, bm, tn), F32)],
        compiler_params=_cparams(("arbitrary", "arbitrary", "arbitrary")),
        name="out_proj",
    )(a1, a2, w, x, gpost, gt, gpre, sc, sh)


FFN_ROW_CHUNK = 64
FFN_DOWN_PIECES = 8


def _ffn_kernel(h_ref, wg_ref, wu_ref, cwb_ref, wo_ref, x_ref, g_ref, gt_ref,
                o_ref, u_even, u_odd, act_even, act_odd, carry_ref):
    i = pl.program_id(1)
    s = pl.program_id(2)
    nj = pl.num_programs(2) - 2
    bm, d = h_ref.shape
    tf = wo_ref.shape[0]
    tile = jnp.minimum(s, nj - 1)
    u_bufs = (u_even, u_odd)
    act_bufs = (act_even, act_odd)

    @pl.when(s == 0)
    def _():
        o_ref[...] = jnp.zeros_like(o_ref)

    @pl.when(jnp.logical_and(i == 0, s < nj))
    def _():
        carry_ref[tile] = jnp.zeros(carry_ref.shape[1:], F32)

    def step(p, up_proj, finish, down_proj):
        u_new, u_last = u_bufs[p], u_bufs[1 - p]
        act_new, act_last = act_bufs[1 - p], act_bufs[p]
        cwb = cwb_ref[...]
        kw = cwb.shape[0] - 1
        cwg, cwu, cbg, cbu = cwb[0:kw, 0:tf], cwb[0:kw, tf:], cwb[kw:, 0:tf], cwb[kw:, tf:]

        def conv(r0, c0, cw, cb):
            cols = slice(c0, c0 + tf)
            return (cb + cw[2:3, :] * u_last[r0:r0 + FFN_ROW_CHUNK, cols]
                    + cw[1:2, :] * u_last[r0 - 1:r0 - 1 + FFN_ROW_CHUNK, cols]
                    + cw[0:1, :] * u_last[r0 - 2:r0 - 2 + FFN_ROW_CHUNK, cols])

        def finish_rows(c, after):
            if after is None:
                zero = 0.0
            else:
                bits = lax.bitcast_convert_type(after[0:1, 0:tf], jnp.uint32)
                zero = lax.shift_right_logical(lax.shift_right_logical(bits, jnp.uint32(16)),
                                               jnp.uint32(16)).astype(F32)
            r0 = SUBLANES + c * FFN_ROW_CHUNK
            gate = conv(r0, 0, cwg, cbg + zero)
            up = conv(r0, tf, cwu, cbu + zero)
            act_new[c * FFN_ROW_CHUNK:(c + 1) * FFN_ROW_CHUNK, :] = (_gelu_tanh(gate) * up).astype(BF16)

        n_chunks = bm // FFN_ROW_CHUNK
        n_pieces = min(FFN_DOWN_PIECES, n_chunks)
        per_piece = n_chunks // n_pieces
        assert per_piece * n_pieces == n_chunks
        nw = d // n_pieces
        act_l = act_last[...] if down_proj else None
        for pc in range(n_pieces):
            piece = None
            if down_proj:
                cols = slice(pc * nw, (pc + 1) * nw)
                piece = jnp.dot(act_l, wo_ref[:, cols], preferred_element_type=F32)
                o_ref[:, cols] += piece
            if finish:
                for c in range(pc * per_piece, (pc + 1) * per_piece):
                    finish_rows(c, piece)

        if up_proj:
            w_gu = jnp.concatenate([wg_ref[...], wu_ref[...]], axis=1)
            u = jnp.dot(h_ref[...], w_gu, preferred_element_type=F32)
            u_new[0:SUBLANES, :] = carry_ref[tile]
            u_new[SUBLANES:, :] = u
            carry_ref[tile] = u[bm - SUBLANES:, :]

    parity = lax.rem(s, 2)
    steady = jnp.logical_and(s >= 2, s < nj)
    variants = [(s == 0, 0, (True, False, False)),
                (s == 1, 1, (True, True, False)),
                (jnp.logical_and(steady, parity == 0), 0, (True, True, True)),
                (jnp.logical_and(steady, parity == 1), 1, (True, True, True)),
                (s == nj, nj % 2, (False, True, True)),
                (s == nj + 1, (nj + 1) % 2, (False, False, True))]
    for cond, p, stages in variants:
        pl.when(cond)(functools.partial(step, p, *stages))

    @pl.when(s == nj + 1)
    def _():
        scale = gt_ref[...] * g_ref[...]
        for r0 in range(0, bm, SUBLANES):
            f = o_ref[r0:r0 + SUBLANES, :]
            o_ref[r0:r0 + SUBLANES, :] = x_ref[r0:r0 + SUBLANES, :] + scale * (f * _rms_scale(f))


def _conv_ffn(h, w_in, cw, cb, wo, x, g, gt, bm, tf):
    bsz, s, d = x.shape
    nj = wo.shape[0] // tf
    kw = cw.shape[0]
    cwb = jnp.concatenate([cw, cb], axis=0).reshape(kw + 1, 2, nj, tf)
    cwb = jnp.transpose(cwb, (2, 0, 1, 3)).reshape(nj, kw + 1, 2 * tf)
    up_t = lambda j: jnp.minimum(j, nj - 1)
    ew_t = lambda j: jnp.clip(j - 1, 0, nj - 1)
    dn_t = lambda j: jnp.clip(j - 2, 0, nj - 1)
    big = lambda: pl.BlockSpec((None, bm, d), lambda b, i, j: (b, i, 0))
    return pl.pallas_call(
        _ffn_kernel,
        out_shape=jax.ShapeDtypeStruct((bsz, s, d), F32),
        grid=(bsz, s // bm, nj + 2),
        in_specs=[pl.BlockSpec((None, bm, d), lambda b, i, j: (b, i, 0), pipeline_mode=pl.Buffered(1)),
                  pl.BlockSpec((d, tf), lambda b, i, j: (0, up_t(j))),
                  pl.BlockSpec((d, tf), lambda b, i, j: (0, nj + up_t(j))),
                  pl.BlockSpec((None, kw + 1, 2 * tf), lambda b, i, j: (ew_t(j), 0, 0)),
                  pl.BlockSpec((tf, d), lambda b, i, j: (dn_t(j), 0)),
                  big(),
                  pl.BlockSpec((1, d), lambda b, i, j: (0, 0)),
                  pl.BlockSpec((None, 1, d), lambda b, i, j: (b, 0, 0))],
        out_specs=big(),
        scratch_shapes=[pltpu.VMEM((SUBLANES + bm, 2 * tf), F32),
                        pltpu.VMEM((SUBLANES + bm, 2 * tf), F32),
                        pltpu.VMEM((bm, tf), BF16),
                        pltpu.VMEM((bm, tf), BF16),
                        pltpu.VMEM((nj, SUBLANES, 2 * tf), F32)],
        compiler_params=_cparams(("arbitrary", "arbitrary", "arbitrary")),
        name="conv_ffn",
    )(h, w_in, w_in, cwb, wo, x, g, gt)


def _layer(x, c, w_ada, b_ada, g_pre_mix, g_post_mix, g_pre_ffn, g_post_ffn, w_in,
           conv_lru_w, conv_lru_b, w_rg_a, b_rg_a, w_rg_x, b_rg_x, lru_lambda,
           g_grp_lru, g_grp_attn, w_out, w_ffn_in, conv_ffn_w, conv_ffn_b, w_ffn_out,
           *, bm_in, bn_in, bt_lru, bq, kb_sz, bm_out, tn_out, bm_ffn, tf):
    bsz, s, d = x.shape
    d_lru = w_rg_a.shape[0] * LRU_BLOCK
    d_attn = g_grp_attn.shape[0]
    d_qi = N_IDX_HEADS * IDX_DIM
    d_kv = (w_in.shape[1] - 2 * d_lru - d_attn - d_qi - IDX_DIM - N_IDX_HEADS) // 2
    row = lambda v: v.reshape(1, -1)

    mod = _modulation(c, w_ada, b_ada)
    sh1, sc1, gt1, sh2, sc2, gt2 = [m.reshape(bsz, 1, d) for m in jnp.split(mod, 6, axis=-1)]

    q_off = 2 * d_lru
    k_off = q_off + d_attn
    qi_off = k_off + 2 * d_kv
    n_main = qi_off + d_qi
    n_small = w_in.shape[1] - n_main
    w_in_b = w_in.astype(BF16)
    w_small = jnp.concatenate([w_in[:, n_main:], jnp.zeros((d, LANES - n_small), F32)], axis=1).astype(BF16)

    proj, small = _in_proj(x, row(g_pre_mix), sc1, sh1, w_in_b, n_main, w_small, bm_in, bn_in)

    y_lru = _rglru(proj, 0, conv_lru_w, row(conv_lru_b), w_rg_a.astype(BF16), row(b_rg_a),
                   w_rg_x.astype(BF16), row(b_rg_x), row(lru_lambda), row(g_grp_lru), bt_lru)

    y_attn = _dsa(proj, q_off, qi_off, k_off, small, row(g_grp_attn), d_attn, d_kv, bq, kb_sz)

    x1, h2 = _out_proj(y_lru, y_attn, w_out.astype(BF16), x, row(g_post_mix), gt1, row(g_pre_ffn), sc2, sh2,
                       bm_out, tn_out)

    d_ff = w_ffn_out.shape[0]
    pad = -d_ff % tf
    halves = lambda a: jnp.concatenate([jnp.pad(a[:, :d_ff], ((0, 0), (0, pad))),
                                        jnp.pad(a[:, d_ff:], ((0, 0), (0, pad)))], axis=1)
    return _conv_ffn(h2, halves(w_ffn_in).astype(BF16), halves(conv_ffn_w), halves(row(conv_ffn_b)),
                     jnp.pad(w_ffn_out, ((0, pad), (0, 0))).astype(BF16), x1, row(g_post_ffn), gt2, bm_ffn, tf)


_TILES = dict(bm_in=512, bn_in=1024, bt_lru=256, bq=256, kb_sz=512, bm_out=512, tn_out=512, bm_ffn=512, tf=256)


def kernel(x, c, w_ada, b_ada, g_pre_mix, g_post_mix, g_pre_ffn, g_post_ffn, w_in, conv_lru_w, conv_lru_b,
           w_rg_a, b_rg_a, w_rg_x, b_rg_x, lru_lambda, g_grp_lru, g_grp_attn, w_out, w_ffn_in,
           conv_ffn_w, conv_ffn_b, w_ffn_out):
    for l in range(w_ada.shape[0]):
        x = _layer(x, c, w_ada[l], b_ada[l], g_pre_mix[l], g_post_mix[l], g_pre_ffn[l], g_post_ffn[l],
                   w_in[l], conv_lru_w[l], conv_lru_b[l], w_rg_a[l], b_rg_a[l], w_rg_x[l], b_rg_x[l],
                   lru_lambda[l], g_grp_lru[l], g_grp_attn[l], w_out[l], w_ffn_in[l], conv_ffn_w[l],
                   conv_ffn_b[l], w_ffn_out[l], **_TILES)
    return x
```

```python
import functools
import math

import jax
import jax.numpy as jnp
from jax import lax
from jax.experimental import pallas as pl
from jax.experimental.pallas import tpu as pltpu

F32 = jnp.float32
BF16 = jnp.bfloat16
I32 = jnp.int32

RMS_EPS = 1e-6
LRU_C = 8.0
LRU_BLOCK = 128
HEAD_DIM = 128
N_IDX_HEADS = 32
IDX_DIM = 64
TOPK_MAX = 256

LANES = 128
SUBLANES = 8
VMEM_LIMIT = 59 * 1024 * 1024

ATTN_HEADS_PER_SWEEP = 4
COUNT_CHAINS = 2
INT_MIN = -(2 ** 31)
NEG_BIG = -1e30
LOG2_E = 1.4426950408889634


def _cparams(sem):
    return pltpu.CompilerParams(dimension_semantics=sem, vmem_limit_bytes=VMEM_LIMIT)


def _rms_scale(v):
    return lax.rsqrt(jnp.mean(v * v, axis=-1, keepdims=True) + RMS_EPS)


def _gelu_tanh(v):
    inner = v * (1.0 + 0.044715 * (v * v))
    return v / (1.0 + jnp.exp((-2.0 * 0.7978845608028654) * inner))


def _shift_rows(cur, prev8, d):
    rolled = pltpu.roll(cur, d, 0)
    head = pltpu.roll(prev8, d, 0)
    row = lax.broadcasted_iota(I32, (SUBLANES, cur.shape[1]), 0)
    top = jnp.where(row < d, head, rolled[0:SUBLANES])
    return jnp.concatenate([top, rolled[SUBLANES:]], axis=0)


def _mod_kernel(c_ref, w_ref, b_ref, o_ref):
    c = c_ref[...]
    s = c * jax.nn.sigmoid(c)
    w = w_ref[...].astype(BF16)
    s_hi = s.astype(BF16)
    s_lo = (s - s_hi.astype(F32)).astype(BF16)
    o_ref[...] = (jnp.dot(s_hi, w, preferred_element_type=F32)
                  + jnp.dot(s_lo, w, preferred_element_type=F32) + b_ref[...])


def _modulation(c, w_ada, b_ada, tn=512):
    bsz, d = c.shape
    n = w_ada.shape[1]
    rows = -(-bsz // SUBLANES) * SUBLANES
    c_pad = jnp.zeros((rows, d), F32).at[:bsz].set(c)
    out = pl.pallas_call(
        _mod_kernel,
        out_shape=jax.ShapeDtypeStruct((rows, n), F32),
        grid=(n // tn,),
        in_specs=[pl.BlockSpec((rows, d), lambda j: (0, 0)),
                  pl.BlockSpec((d, tn), lambda j: (0, j)),
                  pl.BlockSpec((1, tn), lambda j: (0, j))],
        out_specs=pl.BlockSpec((rows, tn), lambda j: (0, j)),
        compiler_params=_cparams(("arbitrary",)),
        name="modulation",
    )(c_pad, w_ada, b_ada.reshape(1, n))
    return out[:bsz]


def _inproj_kernel(x_ref, g_ref, sc_ref, sh_ref, w_ref, ws_ref, o_ref, os_ref, h_ref):
    @pl.when(pl.program_id(2) == 0)
    def _():
        x = x_ref[...]
        y = x * _rms_scale(x) * g_ref[...]
        h = (y * (1.0 + sc_ref[...]) + sh_ref[...]).astype(BF16)
        h_ref[...] = h
        os_ref[...] = jnp.dot(h, ws_ref[...], preferred_element_type=F32)

    o_ref[...] = jnp.dot(h_ref[...], w_ref[...], preferred_element_type=F32).astype(BF16)


def _in_proj(x, g, sc, sh, w_main, n, w_small, bm, bn):
    bsz, s, d = x.shape
    ns = w_small.shape[1]
    return pl.pallas_call(
        _inproj_kernel,
        out_shape=(jax.ShapeDtypeStruct((bsz, s, n), BF16), jax.ShapeDtypeStruct((bsz, s, ns), F32)),
        grid=(bsz, s // bm, n // bn),
        in_specs=[pl.BlockSpec((None, bm, d), lambda b, i, j: (b, i, 0)),
                  pl.BlockSpec((1, d), lambda b, i, j: (0, 0)),
                  pl.BlockSpec((None, 1, d), lambda b, i, j: (b, 0, 0)),
                  pl.BlockSpec((None, 1, d), lambda b, i, j: (b, 0, 0)),
                  pl.BlockSpec((d, bn), lambda b, i, j: (0, j)),
                  pl.BlockSpec((d, ns), lambda b, i, j: (0, 0))],
        out_specs=(pl.BlockSpec((None, bm, bn), lambda b, i, j: (b, i, j)),
                   pl.BlockSpec((None, bm, ns), lambda b, i, j: (b, i, 0))),
        scratch_shapes=[pltpu.VMEM((bm, d), BF16)],
        compiler_params=_cparams(("arbitrary", "arbitrary", "arbitrary")),
        name="in_proj",
    )(x, g, sc, sh, w_main, w_small)


def _rglru_kernel(xl_ref, gate_ref, cw_ref, cb_ref, wa_ref, ba_ref, wx_ref, bx_ref, lam_ref, g_ref,
                  o_ref, halo_ref, hc_ref, a_scr, b_scr):
    bt, dl = xl_ref.shape
    nblk = dl // LRU_BLOCK

    @pl.when(pl.program_id(1) == 0)
    def _():
        halo_ref[...] = jnp.zeros_like(halo_ref)
        hc_ref[...] = jnp.zeros_like(hc_ref)

    xl = xl_ref[...].astype(F32)
    prev = halo_ref[...]
    halo_ref[...] = xl[bt - SUBLANES:, :]

    cw = cw_ref[...]
    kw = cw.shape[0]
    xc = cb_ref[...] + cw[kw - 1:kw, :] * xl
    for dlt in range(1, kw):
        xc = xc + cw[kw - 1 - dlt:kw - dlt, :] * _shift_rows(xl, prev, dlt)

    xcb = xc.astype(BF16)
    r_parts, i_parts = [], []
    for n in range(nblk):
        blk = xcb[:, n * LRU_BLOCK:(n + 1) * LRU_BLOCK]
        r_parts.append(jnp.dot(blk, wa_ref[n], preferred_element_type=F32))
        i_parts.append(jnp.dot(blk, wx_ref[n], preferred_element_type=F32))
    r = jax.nn.sigmoid(jnp.concatenate(r_parts, axis=1) + ba_ref[...])
    ig = jax.nn.sigmoid(jnp.concatenate(i_parts, axis=1) + bx_ref[...])

    z = -lam_ref[...]
    softplus = jnp.maximum(z, 0.0) + jnp.log(1.0 + jnp.exp(-jnp.abs(z)))
    log_a = (-LRU_C * softplus) * r
    a = jnp.exp(log_a)
    u = jnp.sqrt(1.0 - jnp.exp(2.0 * log_a)) * (ig * xc)

    rowmod = lax.broadcasted_iota(I32, (bt, dl), 0) % SUBLANES
    b = u
    for sft in (1, 2, 4):
        keep = rowmod >= sft
        a_prev = jnp.where(keep, pltpu.roll(a, sft, 0), 1.0)
        b_prev = jnp.where(keep, pltpu.roll(b, sft, 0), 0.0)
        b = a * b_prev + b
        a = a * a_prev
    a_scr[...] = a
    b_scr[...] = b

    def group(gi, carry):
        rows = pl.ds(pl.multiple_of(gi * SUBLANES, SUBLANES), SUBLANES)
        h = a_scr[rows, :] * carry + b_scr[rows, :]
        b_scr[rows, :] = h
        return jnp.broadcast_to(h[SUBLANES - 1:SUBLANES, :], (SUBLANES, dl))

    hc_ref[...] = lax.fori_loop(0, bt // SUBLANES, group, hc_ref[...])

    y = b_scr[...] * _gelu_tanh(gate_ref[...].astype(F32))
    o_ref[...] = (y * _rms_scale(y) * g_ref[...]).astype(BF16)


def _rglru(proj, x_off, cw, cb, wa, ba, wx, bx, lam, g, bt):
    bsz, s, _ = proj.shape
    dl = cw.shape[1]
    nblk = dl // LRU_BLOCK
    x_blk = x_off // dl
    vec = lambda: pl.BlockSpec((1, dl), lambda b, t: (0, 0))
    return pl.pallas_call(
        _rglru_kernel,
        out_shape=jax.ShapeDtypeStruct((bsz, s, dl), BF16),
        grid=(bsz, s // bt),
        in_specs=[pl.BlockSpec((None, bt, dl), lambda b, t: (b, t, x_blk)),
                  pl.BlockSpec((None, bt, dl), lambda b, t: (b, t, x_blk + 1)),
                  pl.BlockSpec((cw.shape[0], dl), lambda b, t: (0, 0)),
                  vec(),
                  pl.BlockSpec((nblk, LRU_BLOCK, LRU_BLOCK), lambda b, t: (0, 0, 0)),
                  vec(),
                  pl.BlockSpec((nblk, LRU_BLOCK, LRU_BLOCK), lambda b, t: (0, 0, 0)),
                  vec(), vec(), vec()],
        out_specs=pl.BlockSpec((None, bt, dl), lambda b, t: (b, t, 0)),
        scratch_shapes=[pltpu.VMEM((SUBLANES, dl), F32), pltpu.VMEM((SUBLANES, dl), F32),
                        pltpu.VMEM((bt, dl), F32), pltpu.VMEM((bt, dl), F32)],
        compiler_params=_cparams(("arbitrary", "arbitrary")),
        name="rglru",
    )(proj, proj, cw, cb, wa, ba, wx, bx, lam, g)


def _dsa_kernel(q_ref, *refs, k_sel, rep, n_qi):
    qi_refs = refs[:n_qi]
    (k_ref, v_ref, ki_ref, w_ref, g_ref, o_ref,
     key_scr, bias_scr, m_scr, l_scr, acc_scr, out_scr) = refs[n_qi:]
    bq = q_ref.shape[0]
    nkb_all, kb_sz, _ = key_scr.shape
    n_groups = q_ref.shape[1] // (rep * HEAD_DIM)
    qblk = pl.program_id(1)
    nkb = lax.div((qblk + 1) * bq + kb_sz - 1, kb_sz)
    qi_w = qi_refs[0].shape[1]
    n_idx_heads = n_qi * qi_w // IDX_DIM
    lane_tiles = kb_sz // LANES
    tile = lambda v: jnp.concatenate([v] * lane_tiles, axis=1)
    down = lambda v: jnp.concatenate([v] * (kb_sz // SUBLANES), axis=0)

    w_t = jnp.transpose(w_ref[...])
    w_off = IDX_DIM
    q_heads = []
    for h in range(n_idx_heads):
        piece, col = divmod(h * IDX_DIM, qi_w)
        q_heads.append(qi_refs[piece][:, col:col + IDX_DIM])

    def score_block(kb, carry):
        rows = pl.ds(pl.multiple_of(kb * kb_sz, kb_sz), kb_sz)
        ki = ki_ref[rows, 0:IDX_DIM].astype(BF16)
        acc = jnp.zeros((kb_sz, bq), F32)
        for h, qh in enumerate(q_heads):
            dots = lax.dot_general(ki, qh, (((1,), (1,)), ((), ())), preferred_element_type=F32)
            acc = acc + jnp.maximum(dots, 0.0) * w_t[w_off + h:w_off + h + 1, :]
        bits = lax.bitcast_convert_type(acc, I32)
        key = jnp.where(bits < 0, bits ^ jnp.int32(0x7FFFFFFF), bits)
        kpos = kb * kb_sz + lax.broadcasted_iota(I32, (kb_sz, bq), 0)
        qpos = qblk * bq + lax.broadcasted_iota(I32, (kb_sz, bq), 1)
        key_scr[kb] = jnp.where(kpos <= qpos, key, jnp.int32(INT_MIN))
        return carry

    lax.fori_loop(0, nkb, score_block, 0)

    def count_keys(hit_fn):
        lanes = COUNT_CHAINS

        def body(kb, c):
            ones = jnp.where(hit_fn(kb), 1.0, 0.0)
            return c + jnp.sum(ones.reshape(lanes, kb_sz // (lanes * SUBLANES), SUBLANES, bq), axis=1)
        c = lax.fori_loop(0, nkb, body, jnp.zeros((lanes, SUBLANES, bq), F32))
        return jnp.broadcast_to(jnp.sum(c.reshape(lanes * SUBLANES, bq), axis=0, keepdims=True), (SUBLANES, bq))

    def count_ge(cand):
        cand_rows = down(cand)
        return count_keys(lambda kb: key_scr[kb] >= cand_rows)

    def bit_step(it, carry):
        ans, n_ge = carry
        cand_u = ans | lax.shift_left(jnp.int32(1), 31 - it)
        cnt = count_ge(cand_u ^ jnp.int32(INT_MIN))
        ok = cnt >= k_sel
        return jnp.where(ok, cand_u, ans), jnp.where(ok, cnt, n_ge)

    ans, n_ge = lax.fori_loop(0, 32, bit_step,
                              (jnp.zeros((SUBLANES, bq), I32), jnp.zeros((SUBLANES, bq), F32)))
    thr = jnp.maximum(ans ^ jnp.int32(INT_MIN), jnp.int32(INT_MIN + 1))
    thr_rows = down(thr)

    @pl.when(jnp.max(n_ge) > k_sel)
    def _():
        need = k_sel - count_ge(thr + 1)
        pos_bits = (nkb_all * kb_sz - 1).bit_length()

        def ties_before(pos):
            pos_rows = down(pos)

            def hit(kb):
                kpos = kb * kb_sz + lax.broadcasted_iota(I32, (kb_sz, bq), 0)
                return (key_scr[kb] == thr_rows) & (kpos < pos_rows)
            return count_keys(hit)

        def pos_step(it, last):
            cand = last | lax.shift_left(jnp.int32(1), pos_bits - 1 - it)
            return jnp.where(ties_before(cand) < need, cand, last)

        last_rows = down(lax.fori_loop(0, pos_bits, pos_step, jnp.zeros((SUBLANES, bq), I32)))

        def demote(kb, carry):
            key = key_scr[kb]
            kpos = kb * kb_sz + lax.broadcasted_iota(I32, (kb_sz, bq), 0)
            key_scr[kb] = jnp.where((key == thr_rows) & (kpos > last_rows), key - 1, key)
            return carry

        lax.fori_loop(0, nkb, demote, 0)

    def to_bias(kb, carry):
        bias = jnp.where(key_scr[kb] >= thr_rows, 0.0, NEG_BIG).astype(F32)
        bias_scr[kb] = jnp.transpose(bias)
        return carry

    lax.fori_loop(0, nkb, to_bias, 0)

    c_exp = (HEAD_DIM ** -0.5) * LOG2_E
    sweep = m_scr.shape[0]
    for g0 in range(0, n_groups, sweep):
        heads = range(g0, g0 + sweep)
        qss = [jnp.concatenate(
            [q_ref[:, (g * rep + r) * HEAD_DIM:(g * rep + r + 1) * HEAD_DIM] for r in range(rep)], axis=0)
            for g in heads]
        m_scr[...] = jnp.full_like(m_scr, NEG_BIG)
        l_scr[...] = jnp.zeros_like(l_scr)
        acc_scr[...] = jnp.zeros_like(acc_scr)

        def kv_block(kb, carry):
            rows = pl.ds(pl.multiple_of(kb * kb_sz, kb_sz), kb_sz)
            bias = jnp.concatenate([bias_scr[kb]] * rep, axis=0)
            logits = [lax.dot_general(qs, k_ref[rows, g * HEAD_DIM:(g + 1) * HEAD_DIM],
                                      (((1,), (1,)), ((), ())), preferred_element_type=F32)
                      for qs, g in zip(qss, heads)]
            for slot, (s, g) in enumerate(zip(logits, heads)):
                s = s + bias
                m_prev = m_scr[slot]
                m_new = jnp.maximum(m_prev, jnp.max(s, axis=1, keepdims=True))
                alpha = jnp.exp2((m_prev - m_new) * c_exp)
                p = jnp.exp2((s - tile(m_new)) * c_exp)
                l_scr[slot] = alpha * l_scr[slot] + jnp.sum(p, axis=1, keepdims=True)
                acc_scr[slot] = alpha * acc_scr[slot] + jnp.dot(
                    p.astype(BF16), v_ref[rows, g * HEAD_DIM:(g + 1) * HEAD_DIM], preferred_element_type=F32)
                m_scr[slot] = m_new
            return carry

        lax.fori_loop(0, nkb, kv_block, 0)
        for slot, g in enumerate(heads):
            o = acc_scr[slot] / l_scr[slot]
            for r in range(rep):
                out_scr[:, (g * rep + r) * HEAD_DIM:(g * rep + r + 1) * HEAD_DIM] = o[r * bq:(r + 1) * bq, :]

    y = out_scr[...]
    o_ref[...] = (y * _rms_scale(y) * g_ref[...]).astype(BF16)


def _dsa(proj, q_off, qi_off, k_off, small, g, d_attn, d_kv, bq, kb_sz):
    bsz, s, _ = proj.shape
    nkb_all = s // kb_sz
    rep = d_attn // d_kv
    k_sel = min(TOPK_MAX, s // 4)
    d_qi = N_IDX_HEADS * IDX_DIM
    qi_w = math.gcd(qi_off, d_qi)
    n_qi = d_qi // qi_w
    q_blk = q_off // d_attn
    k_blk = k_off // d_kv
    kern = functools.partial(_dsa_kernel, k_sel=k_sel, rep=rep, n_qi=n_qi)
    qi_specs = [pl.BlockSpec((None, bq, qi_w), lambda b, i, t=t: (b, i, qi_off // qi_w + t)) for t in range(n_qi)]
    return pl.pallas_call(
        kern,
        out_shape=jax.ShapeDtypeStruct((bsz, s, d_attn), BF16),
        grid=(bsz, s // bq),
        in_specs=[pl.BlockSpec((None, bq, d_attn), lambda b, i: (b, i, q_blk))] + qi_specs + [
                  pl.BlockSpec((None, s, d_kv), lambda b, i: (b, 0, k_blk)),
                  pl.BlockSpec((None, s, d_kv), lambda b, i: (b, 0, k_blk + 1)),
                  pl.BlockSpec((None, s, LANES), lambda b, i: (b, 0, 0)),
                  pl.BlockSpec((None, bq, LANES), lambda b, i: (b, i, 0)),
                  pl.BlockSpec((1, d_attn), lambda b, i: (0, 0))],
        out_specs=pl.BlockSpec((None, bq, d_attn), lambda b, i: (b, i, 0)),
        scratch_shapes=[pltpu.VMEM((nkb_all, kb_sz, bq), I32),
                        pltpu.VMEM((nkb_all, bq, kb_sz), F32),
                        pltpu.VMEM((ATTN_HEADS_PER_SWEEP, rep * bq, LANES), F32),
                        pltpu.VMEM((ATTN_HEADS_PER_SWEEP, rep * bq, LANES), F32),
                        pltpu.VMEM((ATTN_HEADS_PER_SWEEP, rep * bq, HEAD_DIM), F32),
                        pltpu.VMEM((bq, d_attn), F32)],
        compiler_params=_cparams(("arbitrary", "arbitrary")),
        name="dsa",
    )(proj, *([proj] * n_qi), proj, proj, small, small, g)


def _outproj_kernel(a1_ref, a2_ref, w_ref, x_ref, gpost_ref, gt_ref, gpre_ref, sc_ref, sh_ref,
                    x1_ref, h2_ref, mix_ref):
    j = pl.program_id(2)
    n_tiles, bm, tn = mix_ref.shape
    k1 = a1_ref.shape[1]
    mix_ref[j] = (jnp.dot(a1_ref[...], w_ref[0:k1, :], preferred_element_type=F32)
                  + jnp.dot(a2_ref[...], w_ref[k1:, :], preferred_element_type=F32))

    @pl.when(j == n_tiles - 1)
    def _():
        post = gt_ref[...] * gpost_ref[...]
        pre = gpre_ref[...] * (1.0 + sc_ref[...])
        sh = sh_ref[...]

        for r0 in range(0, bm, SUBLANES):
            rows = slice(r0, r0 + SUBLANES)
            m = jnp.concatenate([mix_ref[t, rows, :] for t in range(n_tiles)], axis=1)
            x1 = x_ref[rows, :] + post * (m * _rms_scale(m))
            x1_ref[rows, :] = x1
            h2_ref[rows, :] = ((x1 * _rms_scale(x1)) * pre + sh).astype(BF16)


def _out_proj(a1, a2, w, x, gpost, gt, gpre, sc, sh, bm, tn):
    bsz, s, d = x.shape
    k1, k2 = a1.shape[2], a2.shape[2]
    n_tiles = d // tn
    vec = lambda: pl.BlockSpec((1, d), lambda b, i, j: (0, 0))
    bvec = lambda: pl.BlockSpec((None, 1, d), lambda b, i, j: (b, 0, 0))
    big = lambda bufs=None: pl.BlockSpec((None, bm, d), lambda b, i, j: (b, i, 0), pipeline_mode=bufs)
    return pl.pallas_call(
        _outproj_kernel,
        out_shape=(jax.ShapeDtypeStruct((bsz, s, d), F32), jax.ShapeDtypeStruct((bsz, s, d), BF16)),
        grid=(bsz, s // bm, n_tiles),
        in_specs=[pl.BlockSpec((None, bm, k1), lambda b, i, j: (b, i, 0), pipeline_mode=pl.Buffered(1)),
                  pl.BlockSpec((None, bm, k2), lambda b, i, j: (b, i, 0), pipeline_mode=pl.Buffered(1)),
                  pl.BlockSpec((k1 + k2, tn), lambda b, i, j: (0, j)),
                  big(), vec(), bvec(), vec(), bvec(), bvec()],
        out_specs=(big(), big(pl.Buffered(1))),
        scratch_shapes=[pltpu.VMEM((n_tiles, bm, tn), F32)],
        compiler_params=_cparams(("arbitrary", "arbitrary", "arbitrary")),
        name="out_proj",
    )(a1, a2, w, x, gpost, gt, gpre, sc, sh)


FFN_ROW_CHUNK = 64
FFN_DOWN_PIECES = 8


def _ffn_kernel(h_ref, wg_ref, wu_ref, cwb_ref, wo_ref, x_ref, g_ref, gt_ref,
                o_ref, u_even, u_odd, act_even, act_odd, carry_ref):
    i = pl.program_id(1)
    s = pl.program_id(2)
    nj = pl.num_programs(2) - 2
    bm, d = h_ref.shape
    tf = wo_ref.shape[0]
    tile = jnp.minimum(s, nj - 1)
    u_bufs = (u_even, u_odd)
    act_bufs = (act_even, act_odd)

    @pl.when(s == 0)
    def _():
        o_ref[...] = jnp.zeros_like(o_ref)

    @pl.when(jnp.logical_and(i == 0, s < nj))
    def _():
        carry_ref[tile] = jnp.zeros(carry_ref.shape[1:], F32)

    def step(p, up_proj, finish, down_proj):
        u_new, u_last = u_bufs[p], u_bufs[1 - p]
        act_new, act_last = act_bufs[1 - p], act_bufs[p]
        cwb = cwb_ref[...]
        kw = cwb.shape[0] - 1
        cwg, cwu, cbg, cbu = cwb[0:kw, 0:tf], cwb[0:kw, tf:], cwb[kw:, 0:tf], cwb[kw:, tf:]

        def conv(r0, c0, cw, cb):
            cols = slice(c0, c0 + tf)
            return (cb + cw[2:3, :] * u_last[r0:r0 + FFN_ROW_CHUNK, cols]
                    + cw[1:2, :] * u_last[r0 - 1:r0 - 1 + FFN_ROW_CHUNK, cols]
                    + cw[0:1, :] * u_last[r0 - 2:r0 - 2 + FFN_ROW_CHUNK, cols])

        def finish_rows(c, after):
            if after is None:
                zero = 0.0
            else:
                bits = lax.bitcast_convert_type(after[0:1, 0:tf], jnp.uint32)
                zero = lax.shift_right_logical(lax.shift_right_logical(bits, jnp.uint32(16)),
                                               jnp.uint32(16)).astype(F32)
            r0 = SUBLANES + c * FFN_ROW_CHUNK
            gate = conv(r0, 0, cwg, cbg + zero)
            up = conv(r0, tf, cwu, cbu + zero)
            act_new[c * FFN_ROW_CHUNK:(c + 1) * FFN_ROW_CHUNK, :] = (_gelu_tanh(gate) * up).astype(BF16)

        n_chunks = bm // FFN_ROW_CHUNK
        n_pieces = min(FFN_DOWN_PIECES, n_chunks)
        per_piece = n_chunks // n_pieces
        assert per_piece * n_pieces == n_chunks
        nw = d // n_pieces
        act_l = act_last[...] if down_proj else None
        for pc in range(n_pieces):
            piece = None
            if down_proj:
                cols = slice(pc * nw, (pc + 1) * nw)
                piece = jnp.dot(act_l, wo_ref[:, cols], preferred_element_type=F32)
                o_ref[:, cols] += piece
            if finish:
                for c in range(pc * per_piece, (pc + 1) * per_piece):
                    finish_rows(c, piece)

        if up_proj:
            w_gu = jnp.concatenate([wg_ref[...], wu_ref[...]], axis=1)
            u = jnp.dot(h_ref[...], w_gu, preferred_element_type=F32)
            u_new[0:SUBLANES, :] = carry_ref[tile]
            u_new[SUBLANES:, :] = u
            carry_ref[tile] = u[bm - SUBLANES:, :]

    parity = lax.rem(s, 2)
    steady = jnp.logical_and(s >= 2, s < nj)
    variants = [(s == 0, 0, (True, False, False)),
                (s == 1, 1, (True, True, False)),
                (jnp.logical_and(steady, parity == 0), 0, (True, True, True)),
                (jnp.logical_and(steady, parity == 1), 1, (True, True, True)),
                (s == nj, nj % 2, (False, True, True)),
                (s == nj + 1, (nj + 1) % 2, (False, False, True))]
    for cond, p, stages in variants:
        pl.when(cond)(functools.partial(step, p, *stages))

    @pl.when(s == nj + 1)
    def _():
        scale = gt_ref[...] * g_ref[...]
        for r0 in range(0, bm, SUBLANES):
            f = o_ref[r0:r0 + SUBLANES, :]
            o_ref[r0:r0 + SUBLANES, :] = x_ref[r0:r0 + SUBLANES, :] + scale * (f * _rms_scale(f))


def _conv_ffn(h, w_in, cw, cb, wo, x, g, gt, bm, tf):
    bsz, s, d = x.shape
    nj = wo.shape[0] // tf
    kw = cw.shape[0]
    cwb = jnp.concatenate([cw, cb], axis=0).reshape(kw + 1, 2, nj, tf)
    cwb = jnp.transpose(cwb, (2, 0, 1, 3)).reshape(nj, kw + 1, 2 * tf)
    up_t = lambda j: jnp.minimum(j, nj - 1)
    ew_t = lambda j: jnp.clip(j - 1, 0, nj - 1)
    dn_t = lambda j: jnp.clip(j - 2, 0, nj - 1)
    big = lambda: pl.BlockSpec((None, bm, d), lambda b, i, j: (b, i, 0))
    return pl.pallas_call(
        _ffn_kernel,
        out_shape=jax.ShapeDtypeStruct((bsz, s, d), F32),
        grid=(bsz, s // bm, nj + 2),
        in_specs=[pl.BlockSpec((None, bm, d), lambda b, i, j: (b, i, 0), pipeline_mode=pl.Buffered(1)),
                  pl.BlockSpec((d, tf), lambda b, i, j: (0, up_t(j))),
                  pl.BlockSpec((d, tf), lambda b, i, j: (0, nj + up_t(j))),
                  pl.BlockSpec((None, kw + 1, 2 * tf), lambda b, i, j: (ew_t(j), 0, 0)),
                  pl.BlockSpec((tf, d), lambda b, i, j: (dn_t(j), 0)),
                  big(),
                  pl.BlockSpec((1, d), lambda b, i, j: (0, 0)),
                  pl.BlockSpec((None, 1, d), lambda b, i, j: (b, 0, 0))],
        out_specs=big(),
        scratch_shapes=[pltpu.VMEM((SUBLANES + bm, 2 * tf), F32),
                        pltpu.VMEM((SUBLANES + bm, 2 * tf), F32),
                        pltpu.VMEM((bm, tf), BF16),
                        pltpu.VMEM((bm, tf), BF16),
                        pltpu.VMEM((nj, SUBLANES, 2 * tf), F32)],
        compiler_params=_cparams(("arbitrary", "arbitrary", "arbitrary")),
        name="conv_ffn",
    )(h, w_in, w_in, cwb, wo, x, g, gt)


def _layer(x, c, w_ada, b_ada, g_pre_mix, g_post_mix, g_pre_ffn, g_post_ffn, w_in,
           conv_lru_w, conv_lru_b, w_rg_a, b_rg_a, w_rg_x, b_rg_x, lru_lambda,
           g_grp_lru, g_grp_attn, w_out, w_ffn_in, conv_ffn_w, conv_ffn_b, w_ffn_out,
           *, bm_in, bn_in, bt_lru, bq, kb_sz, bm_out, tn_out, bm_ffn, tf):
    bsz, s, d = x.shape
    d_lru = w_rg_a.shape[0] * LRU_BLOCK
    d_attn = g_grp_attn.shape[0]
    d_qi = N_IDX_HEADS * IDX_DIM
    d_kv = (w_in.shape[1] - 2 * d_lru - d_attn - d_qi - IDX_DIM - N_IDX_HEADS) // 2
    row = lambda v: v.reshape(1, -1)

    mod = _modulation(c, w_ada, b_ada)
    sh1, sc1, gt1, sh2, sc2, gt2 = [m.reshape(bsz, 1, d) for m in jnp.split(mod, 6, axis=-1)]

    q_off = 2 * d_lru
    k_off = q_off + d_attn
    qi_off = k_off + 2 * d_kv
    n_main = qi_off + d_qi
    n_small = w_in.shape[1] - n_main
    w_in_b = w_in.astype(BF16)
    w_small = jnp.concatenate([w_in[:, n_main:], jnp.zeros((d, LANES - n_small), F32)], axis=1).astype(BF16)

    proj, small = _in_proj(x, row(g_pre_mix), sc1, sh1, w_in_b, n_main, w_small, bm_in, bn_in)

    y_lru = _rglru(proj, 0, conv_lru_w, row(conv_lru_b), w_rg_a.astype(BF16), row(b_rg_a),
                   w_rg_x.astype(BF16), row(b_rg_x), row(lru_lambda), row(g_grp_lru), bt_lru)

    y_attn = _dsa(proj, q_off, qi_off, k_off, small, row(g_grp_attn), d_attn, d_kv, bq, kb_sz)

    x1, h2 = _out_proj(y_lru, y_attn, w_out.astype(BF16), x, row(g_post_mix), gt1, row(g_pre_ffn), sc2, sh2,
                       bm_out, tn_out)

    d_ff = w_ffn_out.shape[0]
    pad = -d_ff % tf
    halves = lambda a: jnp.concatenate([jnp.pad(a[:, :d_ff], ((0, 0), (0, pad))),
                                        jnp.pad(a[:, d_ff:], ((0, 0), (0, pad)))], axis=1)
    return _conv_ffn(h2, halves(w_ffn_in).astype(BF16), halves(conv_ffn_w), halves(row(conv_ffn_b)),
                     jnp.pad(w_ffn_out, ((0, pad), (0, 0))).astype(BF16), x1, row(g_post_ffn), gt2, bm_ffn, tf)


_TILES = dict(bm_in=512, bn_in=1536, bt_lru=256, bq=256, kb_sz=512, bm_out=512, tn_out=512, bm_ffn=512, tf=256)


def kernel(x, c, w_ada, b_ada, g_pre_mix, g_post_mix, g_pre_ffn, g_post_ffn, w_in, conv_lru_w, conv_lru_b,
           w_rg_a, b_rg_a, w_rg_x, b_rg_x, lru_lambda, g_grp_lru, g_grp_attn, w_out, w_ffn_in,
           conv_ffn_w, conv_ffn_b, w_ffn_out):
    for l in range(w_ada.shape[0]):
        x = _layer(x, c, w_ada[l], b_ada[l], g_pre_mix[l], g_post_mix[l], g_pre_ffn[l], g_post_ffn[l],
                   w_in[l], conv_lru_w[l], conv_lru_b[l], w_rg_a[l], b_rg_a[l], w_rg_x[l], b_rg_x[l],
                   lru_lambda[l], g_grp_lru[l], g_grp_attn[l], w_out[l], w_ffn_in[l], conv_ffn_w[l],
                   conv_ffn_b[l], w_ffn_out[l], **_TILES)
    return x
```

```python
import functools
import math

import jax
import jax.numpy as jnp
from jax import lax
from jax.experimental import pallas as pl
from jax.experimental.pallas import tpu as pltpu

F32 = jnp.float32
BF16 = jnp.bfloat16
I32 = jnp.int32

RMS_EPS = 1e-6
LRU_C = 8.0
LRU_BLOCK = 128
HEAD_DIM = 128
N_IDX_HEADS = 32
IDX_DIM = 64
TOPK_MAX = 256

LANES = 128
SUBLANES = 8
VMEM_LIMIT = 59 * 1024 * 1024

ATTN_HEADS_PER_SWEEP = 4
COUNT_CHAINS = 2
INT_MIN = -(2 ** 31)
NEG_BIG = -1e30
LOG2_E = 1.4426950408889634


def _cparams(sem):
    return pltpu.CompilerParams(dimension_semantics=sem, vmem_limit_bytes=VMEM_LIMIT)


def _rms_scale(v):
    return lax.rsqrt(jnp.mean(v * v, axis=-1, keepdims=True) + RMS_EPS)


def _gelu_tanh(v):
    inner = v * (1.0 + 0.044715 * (v * v))
    return v / (1.0 + jnp.exp((-2.0 * 0.7978845608028654) * inner))


def _shift_rows(cur, prev8, d):
    rolled = pltpu.roll(cur, d, 0)
    head = pltpu.roll(prev8, d, 0)
    row = lax.broadcasted_iota(I32, (SUBLANES, cur.shape[1]), 0)
    top = jnp.where(row < d, head, rolled[0:SUBLANES])
    return jnp.concatenate([top, rolled[SUBLANES:]], axis=0)


def _mod_kernel(c_ref, w_ref, b_ref, o_ref):
    c = c_ref[...]
    s = c * jax.nn.sigmoid(c)
    w = w_ref[...].astype(BF16)
    s_hi = s.astype(BF16)
    s_lo = (s - s_hi.astype(F32)).astype(BF16)
    o_ref[...] = (jnp.dot(s_hi, w, preferred_element_type=F32)
                  + jnp.dot(s_lo, w, preferred_element_type=F32) + b_ref[...])


def _modulation(c, w_ada, b_ada, tn=512):
    bsz, d = c.shape
    n = w_ada.shape[1]
    rows = -(-bsz // SUBLANES) * SUBLANES
    c_pad = jnp.zeros((rows, d), F32).at[:bsz].set(c)
    out = pl.pallas_call(
        _mod_kernel,
        out_shape=jax.ShapeDtypeStruct((rows, n), F32),
        grid=(n // tn,),
        in_specs=[pl.BlockSpec((rows, d), lambda j: (0, 0)),
                  pl.BlockSpec((d, tn), lambda j: (0, j)),
                  pl.BlockSpec((1, tn), lambda j: (0, j))],
        out_specs=pl.BlockSpec((rows, tn), lambda j: (0, j)),
        compiler_params=_cparams(("arbitrary",)),
        name="modulation",
    )(c_pad, w_ada, b_ada.reshape(1, n))
    return out[:bsz]


def _inproj_kernel(x_ref, g_ref, sc_ref, sh_ref, w_ref, ws_ref, o_ref, os_ref, h_ref):
    @pl.when(pl.program_id(2) == 0)
    def _():
        x = x_ref[...]
        y = x * _rms_scale(x) * g_ref[...]
        h = (y * (1.0 + sc_ref[...]) + sh_ref[...]).astype(BF16)
        h_ref[...] = h
        os_ref[...] = jnp.dot(h, ws_ref[...], preferred_element_type=F32)

    o_ref[...] = jnp.dot(h_ref[...], w_ref[...], preferred_element_type=F32).astype(BF16)


def _in_proj(x, g, sc, sh, w_main, n, w_small, bm, bn):
    bsz, s, d = x.shape
    ns = w_small.shape[1]
    return pl.pallas_call(
        _inproj_kernel,
        out_shape=(jax.ShapeDtypeStruct((bsz, s, n), BF16), jax.ShapeDtypeStruct((bsz, s, ns), F32)),
        grid=(bsz, s // bm, n // bn),
        in_specs=[pl.BlockSpec((None, bm, d), lambda b, i, j: (b, i, 0)),
                  pl.BlockSpec((1, d), lambda b, i, j: (0, 0)),
                  pl.BlockSpec((None, 1, d), lambda b, i, j: (b, 0, 0)),
                  pl.BlockSpec((None, 1, d), lambda b, i, j: (b, 0, 0)),
                  pl.BlockSpec((d, bn), lambda b, i, j: (0, j)),
                  pl.BlockSpec((d, ns), lambda b, i, j: (0, 0))],
        out_specs=(pl.BlockSpec((None, bm, bn), lambda b, i, j: (b, i, j)),
                   pl.BlockSpec((None, bm, ns), lambda b, i, j: (b, i, 0))),
        scratch_shapes=[pltpu.VMEM((bm, d), BF16)],
        compiler_params=_cparams(("arbitrary", "arbitrary", "arbitrary")),
        name="in_proj",
    )(x, g, sc, sh, w_main, w_small)


def _rglru_kernel(xl_ref, gate_ref, cw_ref, cb_ref, wa_ref, ba_ref, wx_ref, bx_ref, lam_ref, g_ref,
                  o_ref, halo_ref, hc_ref, a_scr, b_scr):
    bt, dl = xl_ref.shape
    nblk = dl // LRU_BLOCK

    @pl.when(pl.program_id(1) == 0)
    def _():
        halo_ref[...] = jnp.zeros_like(halo_ref)
        hc_ref[...] = jnp.zeros_like(hc_ref)

    xl = xl_ref[...].astype(F32)
    prev = halo_ref[...]
    halo_ref[...] = xl[bt - SUBLANES:, :]

    cw = cw_ref[...]
    kw = cw.shape[0]
    xc = cb_ref[...] + cw[kw - 1:kw, :] * xl
    for dlt in range(1, kw):
        xc = xc + cw[kw - 1 - dlt:kw - dlt, :] * _shift_rows(xl, prev, dlt)

    xcb = xc.astype(BF16)
    r_parts, i_parts = [], []
    for n in range(nblk):
        blk = xcb[:, n * LRU_BLOCK:(n + 1) * LRU_BLOCK]
        r_parts.append(jnp.dot(blk, wa_ref[n], preferred_element_type=F32))
        i_parts.append(jnp.dot(blk, wx_ref[n], preferred_element_type=F32))
    r = jax.nn.sigmoid(jnp.concatenate(r_parts, axis=1) + ba_ref[...])
    ig = jax.nn.sigmoid(jnp.concatenate(i_parts, axis=1) + bx_ref[...])

    z = -lam_ref[...]
    softplus = jnp.maximum(z, 0.0) + jnp.log(1.0 + jnp.exp(-jnp.abs(z)))
    log_a = (-LRU_C * softplus) * r
    a = jnp.exp(log_a)
    u = jnp.sqrt(1.0 - jnp.exp(2.0 * log_a)) * (ig * xc)

    rowmod = lax.broadcasted_iota(I32, (bt, dl), 0) % SUBLANES
    b = u
    for sft in (1, 2, 4):
        keep = rowmod >= sft
        a_prev = jnp.where(keep, pltpu.roll(a, sft, 0), 1.0)
        b_prev = jnp.where(keep, pltpu.roll(b, sft, 0), 0.0)
        b = a * b_prev + b
        a = a * a_prev
    a_scr[...] = a
    b_scr[...] = b

    def group(gi, carry):
        rows = pl.ds(pl.multiple_of(gi * SUBLANES, SUBLANES), SUBLANES)
        h = a_scr[rows, :] * carry + b_scr[rows, :]
        b_scr[rows, :] = h
        return jnp.broadcast_to(h[SUBLANES - 1:SUBLANES, :], (SUBLANES, dl))

    hc_ref[...] = lax.fori_loop(0, bt // SUBLANES, group, hc_ref[...])

    y = b_scr[...] * _gelu_tanh(gate_ref[...].astype(F32))
    o_ref[...] = (y * _rms_scale(y) * g_ref[...]).astype(BF16)


def _rglru(proj, x_off, cw, cb, wa, ba, wx, bx, lam, g, bt):
    bsz, s, _ = proj.shape
    dl = cw.shape[1]
    nblk = dl // LRU_BLOCK
    x_blk = x_off // dl
    vec = lambda: pl.BlockSpec((1, dl), lambda b, t: (0, 0))
    return pl.pallas_call(
        _rglru_kernel,
        out_shape=jax.ShapeDtypeStruct((bsz, s, dl), BF16),
        grid=(bsz, s // bt),
        in_specs=[pl.BlockSpec((None, bt, dl), lambda b, t: (b, t, x_blk)),
                  pl.BlockSpec((None, bt, dl), lambda b, t: (b, t, x_blk + 1)),
                  pl.BlockSpec((cw.shape[0], dl), lambda b, t: (0, 0)),
                  vec(),
                  pl.BlockSpec((nblk, LRU_BLOCK, LRU_BLOCK), lambda b, t: (0, 0, 0)),
                  vec(),
                  pl.BlockSpec((nblk, LRU_BLOCK, LRU_BLOCK), lambda b, t: (0, 0, 0)),
                  vec(), vec(), vec()],
        out_specs=pl.BlockSpec((None, bt, dl), lambda b, t: (b, t, 0)),
        scratch_shapes=[pltpu.VMEM((SUBLANES, dl), F32), pltpu.VMEM((SUBLANES, dl), F32),
                        pltpu.VMEM((bt, dl), F32), pltpu.VMEM((bt, dl), F32)],
        compiler_params=_cparams(("arbitrary", "arbitrary")),
        name="rglru",
    )(proj, proj, cw, cb, wa, ba, wx, bx, lam, g)


def _dsa_kernel(q_ref, *refs, k_sel, rep, n_qi):
    qi_refs = refs[:n_qi]
    (k_ref, v_ref, ki_ref, w_ref, g_ref, o_ref,
     key_scr, bias_scr, m_scr, l_scr, acc_scr, out_scr) = refs[n_qi:]
    bq = q_ref.shape[0]
    nkb_all, kb_sz, _ = key_scr.shape
    n_groups = q_ref.shape[1] // (rep * HEAD_DIM)
    qblk = pl.program_id(1)
    nkb = lax.div((qblk + 1) * bq + kb_sz - 1, kb_sz)
    qi_w = qi_refs[0].shape[1]
    n_idx_heads = n_qi * qi_w // IDX_DIM
    lane_tiles = kb_sz // LANES
    tile = lambda v: jnp.concatenate([v] * lane_tiles, axis=1)
    down = lambda v: jnp.concatenate([v] * (kb_sz // SUBLANES), axis=0)

    w_t = jnp.transpose(w_ref[...])
    w_off = IDX_DIM
    q_heads = []
    for h in range(n_idx_heads):
        piece, col = divmod(h * IDX_DIM, qi_w)
        q_heads.append(qi_refs[piece][:, col:col + IDX_DIM])

    def score_block(kb, carry):
        rows = pl.ds(pl.multiple_of(kb * kb_sz, kb_sz), kb_sz)
        ki = ki_ref[rows, 0:IDX_DIM].astype(BF16)
        acc = jnp.zeros((kb_sz, bq), F32)
        for h, qh in enumerate(q_heads):
            dots = lax.dot_general(ki, qh, (((1,), (1,)), ((), ())), preferred_element_type=F32)
            acc = acc + jnp.maximum(dots, 0.0) * w_t[w_off + h:w_off + h + 1, :]
        bits = lax.bitcast_convert_type(acc, I32)
        key = jnp.where(bits < 0, bits ^ jnp.int32(0x7FFFFFFF), bits)
        kpos = kb * kb_sz + lax.broadcasted_iota(I32, (kb_sz, bq), 0)
        qpos = qblk * bq + lax.broadcasted_iota(I32, (kb_sz, bq), 1)
        key_scr[kb] = jnp.where(kpos <= qpos, key, jnp.int32(INT_MIN))
        return carry

    lax.fori_loop(0, nkb, score_block, 0)

    def count_keys(hit_fn):
        lanes = COUNT_CHAINS

        def body(kb, c):
            ones = jnp.where(hit_fn(kb), 1.0, 0.0)
            return c + jnp.sum(ones.reshape(lanes, kb_sz // (lanes * SUBLANES), SUBLANES, bq), axis=1)
        c = lax.fori_loop(0, nkb, body, jnp.zeros((lanes, SUBLANES, bq), F32))
        return jnp.broadcast_to(jnp.sum(c.reshape(lanes * SUBLANES, bq), axis=0, keepdims=True), (SUBLANES, bq))

    def count_ge(cand):
        cand_rows = down(cand)
        return count_keys(lambda kb: key_scr[kb] >= cand_rows)

    def bit_step(it, carry):
        ans, n_ge = carry
        cand_u = ans | lax.shift_left(jnp.int32(1), 31 - it)
        cnt = count_ge(cand_u ^ jnp.int32(INT_MIN))
        ok = cnt >= k_sel
        return jnp.where(ok, cand_u, ans), jnp.where(ok, cnt, n_ge)

    ans, n_ge = lax.fori_loop(0, 32, bit_step,
                              (jnp.zeros((SUBLANES, bq), I32), jnp.zeros((SUBLANES, bq), F32)))
    thr = jnp.maximum(ans ^ jnp.int32(INT_MIN), jnp.int32(INT_MIN + 1))
    thr_rows = down(thr)

    @pl.when(jnp.max(n_ge) > k_sel)
    def _():
        need = k_sel - count_ge(thr + 1)
        pos_bits = (nkb_all * kb_sz - 1).bit_length()

        def ties_before(pos):
            pos_rows = down(pos)

            def hit(kb):
                kpos = kb * kb_sz + lax.broadcasted_iota(I32, (kb_sz, bq), 0)
                return (key_scr[kb] == thr_rows) & (kpos < pos_rows)
            return count_keys(hit)

        def pos_step(it, last):
            cand = last | lax.shift_left(jnp.int32(1), pos_bits - 1 - it)
            return jnp.where(ties_before(cand) < need, cand, last)

        last_rows = down(lax.fori_loop(0, pos_bits, pos_step, jnp.zeros((SUBLANES, bq), I32)))

        def demote(kb, carry):
            key = key_scr[kb]
            kpos = kb * kb_sz + lax.broadcasted_iota(I32, (kb_sz, bq), 0)
            key_scr[kb] = jnp.where((key == thr_rows) & (kpos > last_rows), key - 1, key)
            return carry

        lax.fori_loop(0, nkb, demote, 0)

    def to_bias(kb, carry):
        bias = jnp.where(key_scr[kb] >= thr_rows, 0.0, NEG_BIG).astype(F32)
        bias_scr[kb] = jnp.transpose(bias)
        return carry

    lax.fori_loop(0, nkb, to_bias, 0)

    c_exp = (HEAD_DIM ** -0.5) * LOG2_E
    sweep = m_scr.shape[0]
    for g0 in range(0, n_groups, sweep):
        heads = range(g0, g0 + sweep)
        qss = [jnp.concatenate(
            [q_ref[:, (g * rep + r) * HEAD_DIM:(g * rep + r + 1) * HEAD_DIM] for r in range(rep)], axis=0)
            for g in heads]
        m_scr[...] = jnp.full_like(m_scr, NEG_BIG)
        l_scr[...] = jnp.zeros_like(l_scr)
        acc_scr[...] = jnp.zeros_like(acc_scr)

        def kv_block(kb, carry):
            rows = pl.ds(pl.multiple_of(kb * kb_sz, kb_sz), kb_sz)
            bias = jnp.concatenate([bias_scr[kb]] * rep, axis=0)
            logits = [lax.dot_general(qs, k_ref[rows, g * HEAD_DIM:(g + 1) * HEAD_DIM],
                                      (((1,), (1,)), ((), ())), preferred_element_type=F32)
                      for qs, g in zip(qss, heads)]
            for slot, (s, g) in enumerate(zip(logits, heads)):
                s = s + bias
                m_prev = m_scr[slot]
                m_new = jnp.maximum(m_prev, jnp.max(s, axis=1, keepdims=True))
                alpha = jnp.exp2((m_prev - m_new) * c_exp)
                p = jnp.exp2((s - tile(m_new)) * c_exp)
                l_scr[slot] = alpha * l_scr[slot] + jnp.sum(p, axis=1, keepdims=True)
                acc_scr[slot] = alpha * acc_scr[slot] + jnp.dot(
                    p.astype(BF16), v_ref[rows, g * HEAD_DIM:(g + 1) * HEAD_DIM], preferred_element_type=F32)
                m_scr[slot] = m_new
            return carry

        lax.fori_loop(0, nkb, kv_block, 0)
        for slot, g in enumerate(heads):
            o = acc_scr[slot] / l_scr[slot]
            for r in range(rep):
                out_scr[:, (g * rep + r) * HEAD_DIM:(g * rep + r + 1) * HEAD_DIM] = o[r * bq:(r + 1) * bq, :]

    y = out_scr[...]
    o_ref[...] = (y * _rms_scale(y) * g_ref[...]).astype(BF16)


def _dsa(proj, q_off, qi_off, k_off, small, g, d_attn, d_kv, bq, kb_sz):
    bsz, s, _ = proj.shape
    nkb_all = s // kb_sz
    rep = d_attn // d_kv
    k_sel = min(TOPK_MAX, s // 4)
    d_qi = N_IDX_HEADS * IDX_DIM
    qi_w = math.gcd(qi_off, d_qi)
    n_qi = d_qi // qi_w
    q_blk = q_off // d_attn
    k_blk = k_off // d_kv
    kern = functools.partial(_dsa_kernel, k_sel=k_sel, rep=rep, n_qi=n_qi)
    qi_specs = [pl.BlockSpec((None, bq, qi_w), lambda b, i, t=t: (b, i, qi_off // qi_w + t)) for t in range(n_qi)]
    return pl.pallas_call(
        kern,
        out_shape=jax.ShapeDtypeStruct((bsz, s, d_attn), BF16),
        grid=(bsz, s // bq),
        in_specs=[pl.BlockSpec((None, bq, d_attn), lambda b, i: (b, i, q_blk))] + qi_specs + [
                  pl.BlockSpec((None, s, d_kv), lambda b, i: (b, 0, k_blk)),
                  pl.BlockSpec((None, s, d_kv), lambda b, i: (b, 0, k_blk + 1)),
                  pl.BlockSpec((None, s, LANES), lambda b, i: (b, 0, 0)),
                  pl.BlockSpec((None, bq, LANES), lambda b, i: (b, i, 0)),
                  pl.BlockSpec((1, d_attn), lambda b, i: (0, 0))],
        out_specs=pl.BlockSpec((None, bq, d_attn), lambda b, i: (b, i, 0)),
        scratch_shapes=[pltpu.VMEM((nkb_all, kb_sz, bq), I32),
                        pltpu.VMEM((nkb_all, bq, kb_sz), F32),
                        pltpu.VMEM((ATTN_HEADS_PER_SWEEP, rep * bq, LANES), F32),
                        pltpu.VMEM((ATTN_HEADS_PER_SWEEP, rep * bq, LANES), F32),
                        pltpu.VMEM((ATTN_HEADS_PER_SWEEP, rep * bq, HEAD_DIM), F32),
                        pltpu.VMEM((bq, d_attn), F32)],
        compiler_params=_cparams(("arbitrary", "arbitrary")),
        name="dsa",
    )(proj, *([proj] * n_qi), proj, proj, small, small, g)


def _outproj_kernel(a1_ref, a2_ref, w_ref, x_ref, gpost_ref, gt_ref, gpre_ref, sc_ref, sh_ref,
                    x1_ref, h2_ref, mix_ref):
    j = pl.program_id(2)
    n_tiles, bm, tn = mix_ref.shape
    k1 = a1_ref.shape[1]
    mix_ref[j] = (jnp.dot(a1_ref[...], w_ref[0:k1, :], preferred_element_type=F32)
                  + jnp.dot(a2_ref[...], w_ref[k1:, :], preferred_element_type=F32))

    @pl.when(j == n_tiles - 1)
    def _():
        post = gt_ref[...] * gpost_ref[...]
        pre = gpre_ref[...] * (1.0 + sc_ref[...])
        sh = sh_ref[...]

        for r0 in range(0, bm, SUBLANES):
            rows = slice(r0, r0 + SUBLANES)
            m = jnp.concatenate([mix_ref[t, rows, :] for t in range(n_tiles)], axis=1)
            x1 = x_ref[rows, :] + post * (m * _rms_scale(m))
            x1_ref[rows, :] = x1
            h2_ref[rows, :] = ((x1 * _rms_scale(x1)) * pre + sh).astype(BF16)


def _out_proj(a1, a2, w, x, gpost, gt, gpre, sc, sh, bm, tn):
    bsz, s, d = x.shape
    k1, k2 = a1.shape[2], a2.shape[2]
    n_tiles = d // tn
    vec = lambda: pl.BlockSpec((1, d), lambda b, i, j: (0, 0))
    bvec = lambda: pl.BlockSpec((None, 1, d), lambda b, i, j: (b, 0, 0))
    big = lambda bufs=None: pl.BlockSpec((None, bm, d), lambda b, i, j: (b, i, 0), pipeline_mode=bufs)
    return pl.pallas_call(
        _outproj_kernel,
        out_shape=(jax.ShapeDtypeStruct((bsz, s, d), F32), jax.ShapeDtypeStruct((bsz, s, d), BF16)),
        grid=(bsz, s // bm, n_tiles),
        in_specs=[pl.BlockSpec((None, bm, k1), lambda b, i, j: (b, i, 0), pipeline_mode=pl.Buffered(1)),
                  pl.BlockSpec((None, bm, k2), lambda b, i, j: (b, i, 0), pipeline_mode=pl.Buffered(1)),
                  pl.BlockSpec((k1 + k2, tn), lambda b, i, j: (0, j)),
                  big(), vec(), bvec(), vec(), bvec(), bvec()],
        out_specs=(big(), big(pl.Buffered(1))),
        scratch_shapes=[pltpu.VMEM((n_tiles, bm, tn), F32)],
        compiler_params=_cparams(("arbitrary", "arbitrary", "arbitrary")),
        name="out_proj",
    )(a1, a2, w, x, gpost, gt, gpre, sc, sh)


FFN_ROW_CHUNK = 64
FFN_DOWN_PIECES = 8


def _ffn_kernel(h_ref, wg0, wu0, cwb0, wo0, wg1, wu1, cwb1, wo1, x_ref, g_ref, gt_ref,
                o_ref, u_even, u_odd, act_even, act_odd, carry_ref, *, nj):
    first = 2 * pl.program_id(2)
    _ffn_step(first, 0, h_ref, wg0, wu0, cwb0, wo0, x_ref, g_ref, gt_ref,
              o_ref, u_even, u_odd, act_even, act_odd, carry_ref, nj)
    _ffn_step(first + 1, 1, h_ref, wg1, wu1, cwb1, wo1, x_ref, g_ref, gt_ref,
              o_ref, u_even, u_odd, act_even, act_odd, carry_ref, nj)


def _ffn_step(s, parity, h_ref, wg_ref, wu_ref, cwb_ref, wo_ref, x_ref, g_ref, gt_ref,
              o_ref, u_even, u_odd, act_even, act_odd, carry_ref, nj):
    i = pl.program_id(1)
    bm, d = h_ref.shape
    tf = wo_ref.shape[0]
    tile = jnp.minimum(s, nj - 1)
    u_bufs = (u_even, u_odd)
    act_bufs = (act_even, act_odd)

    @pl.when(s == 0)
    def _():
        o_ref[...] = jnp.zeros_like(o_ref)

    @pl.when(jnp.logical_and(i == 0, s < nj))
    def _():
        carry_ref[tile] = jnp.zeros(carry_ref.shape[1:], F32)

    def step(p, up_proj, finish, down_proj):
        u_new, u_last = u_bufs[p], u_bufs[1 - p]
        act_new, act_last = act_bufs[1 - p], act_bufs[p]
        cwb = cwb_ref[...]
        kw = cwb.shape[0] - 1
        cwg, cwu, cbg, cbu = cwb[0:kw, 0:tf], cwb[0:kw, tf:], cwb[kw:, 0:tf], cwb[kw:, tf:]

        def conv(r0, c0, cw, cb):
            cols = slice(c0, c0 + tf)
            return (cb + cw[2:3, :] * u_last[r0:r0 + FFN_ROW_CHUNK, cols]
                    + cw[1:2, :] * u_last[r0 - 1:r0 - 1 + FFN_ROW_CHUNK, cols]
                    + cw[0:1, :] * u_last[r0 - 2:r0 - 2 + FFN_ROW_CHUNK, cols])

        def finish_rows(c, after):
            if after is None:
                zero = 0.0
            else:
                bits = lax.bitcast_convert_type(after[0:1, 0:tf], jnp.uint32)
                zero = lax.shift_right_logical(lax.shift_right_logical(bits, jnp.uint32(16)),
                                               jnp.uint32(16)).astype(F32)
            r0 = SUBLANES + c * FFN_ROW_CHUNK
            gate = conv(r0, 0, cwg, cbg + zero)
            up = conv(r0, tf, cwu, cbu + zero)
            act_new[c * FFN_ROW_CHUNK:(c + 1) * FFN_ROW_CHUNK, :] = (_gelu_tanh(gate) * up).astype(BF16)

        n_chunks = bm // FFN_ROW_CHUNK
        n_pieces = min(FFN_DOWN_PIECES, n_chunks)
        per_piece = n_chunks // n_pieces
        assert per_piece * n_pieces == n_chunks
        nw = d // n_pieces
        act_l = act_last[...] if down_proj else None
        for pc in range(n_pieces):
            piece = None
            if down_proj:
                cols = slice(pc * nw, (pc + 1) * nw)
                piece = jnp.dot(act_l, wo_ref[:, cols], preferred_element_type=F32)
                o_ref[:, cols] += piece
            if finish:
                for c in range(pc * per_piece, (pc + 1) * per_piece):
                    finish_rows(c, piece)

        if up_proj:
            w_gu = jnp.concatenate([wg_ref[...], wu_ref[...]], axis=1)
            u = jnp.dot(h_ref[...], w_gu, preferred_element_type=F32)
            u_new[0:SUBLANES, :] = carry_ref[tile]
            u_new[SUBLANES:, :] = u
            carry_ref[tile] = u[bm - SUBLANES:, :]

    steady = jnp.logical_and(s >= 2, s < nj)
    variants = [(s == 0, 0, (True, False, False)),
                (s == 1, 1, (True, True, False)),
                (steady, parity, (True, True, True)),
                (s == nj, nj % 2, (False, True, True)),
                (s == nj + 1, (nj + 1) % 2, (False, False, True))]
    for cond, p, stages in variants:
        if p == parity:
            pl.when(cond)(functools.partial(step, p, *stages))

    if (nj + 1) % 2 == parity:
        @pl.when(s == nj + 1)
        def _():
            scale = gt_ref[...] * g_ref[...]
            for r0 in range(0, bm, SUBLANES):
                f = o_ref[r0:r0 + SUBLANES, :]
                o_ref[r0:r0 + SUBLANES, :] = x_ref[r0:r0 + SUBLANES, :] + scale * (f * _rms_scale(f))


def _conv_ffn(h, w_in, cw, cb, wo, x, g, gt, bm, tf):
    bsz, s, d = x.shape
    nj = wo.shape[0] // tf
    kw = cw.shape[0]
    cwb = jnp.concatenate([cw, cb], axis=0).reshape(kw + 1, 2, nj, tf)
    cwb = jnp.transpose(cwb, (2, 0, 1, 3)).reshape(nj, kw + 1, 2 * tf)
    up_t = lambda j: jnp.minimum(j, nj - 1)
    ew_t = lambda j: jnp.clip(j - 1, 0, nj - 1)
    dn_t = lambda j: jnp.clip(j - 2, 0, nj - 1)
    big = lambda: pl.BlockSpec((None, bm, d), lambda b, i, j: (b, i, 0), pipeline_mode=pl.Buffered(1))

    def tile_specs(e):
        return [pl.BlockSpec((d, tf), lambda b, i, j: (0, up_t(2 * j + e))),
                pl.BlockSpec((d, tf), lambda b, i, j: (0, nj + up_t(2 * j + e))),
                pl.BlockSpec((None, kw + 1, 2 * tf), lambda b, i, j: (ew_t(2 * j + e), 0, 0)),
                pl.BlockSpec((tf, d), lambda b, i, j: (dn_t(2 * j + e), 0))]

    return pl.pallas_call(
        functools.partial(_ffn_kernel, nj=nj),
        out_shape=jax.ShapeDtypeStruct((bsz, s, d), F32),
        grid=(bsz, s // bm, (nj + 3) // 2),
        in_specs=[big()] + tile_specs(0) + tile_specs(1) + [
                  big(),
                  pl.BlockSpec((1, d), lambda b, i, j: (0, 0)),
                  pl.BlockSpec((None, 1, d), lambda b, i, j: (b, 0, 0))],
        out_specs=big(),
        scratch_shapes=[pltpu.VMEM((SUBLANES + bm, 2 * tf), F32),
                        pltpu.VMEM((SUBLANES + bm, 2 * tf), F32),
                        pltpu.VMEM((bm, tf), BF16),
                        pltpu.VMEM((bm, tf), BF16),
                        pltpu.VMEM((nj, SUBLANES, 2 * tf), F32)],
        compiler_params=_cparams(("arbitrary", "arbitrary", "arbitrary")),
        name="conv_ffn",
    )(h, w_in, w_in, cwb, wo, w_in, w_in, cwb, wo, x, g, gt)


def _layer(x, c, w_ada, b_ada, g_pre_mix, g_post_mix, g_pre_ffn, g_post_ffn, w_in,
           conv_lru_w, conv_lru_b, w_rg_a, b_rg_a, w_rg_x, b_rg_x, lru_lambda,
           g_grp_lru, g_grp_attn, w_out, w_ffn_in, conv_ffn_w, conv_ffn_b, w_ffn_out,
           *, bm_in, bn_in, bt_lru, bq, kb_sz, bm_out, tn_out, bm_ffn, tf):
    bsz, s, d = x.shape
    d_lru = w_rg_a.shape[0] * LRU_BLOCK
    d_attn = g_grp_attn.shape[0]
    d_qi = N_IDX_HEADS * IDX_DIM
    d_kv = (w_in.shape[1] - 2 * d_lru - d_attn - d_qi - IDX_DIM - N_IDX_HEADS) // 2
    row = lambda v: v.reshape(1, -1)

    mod = _modulation(c, w_ada, b_ada)
    sh1, sc1, gt1, sh2, sc2, gt2 = [m.reshape(bsz, 1, d) for m in jnp.split(mod, 6, axis=-1)]

    q_off = 2 * d_lru
    k_off = q_off + d_attn
    qi_off = k_off + 2 * d_kv
    n_main = qi_off + d_qi
    n_small = w_in.shape[1] - n_main
    w_in_b = w_in.astype(BF16)
    w_small = jnp.concatenate([w_in[:, n_main:], jnp.zeros((d, LANES - n_small), F32)], axis=1).astype(BF16)

    proj, small = _in_proj(x, row(g_pre_mix), sc1, sh1, w_in_b, n_main, w_small, bm_in, bn_in)

    y_lru = _rglru(proj, 0, conv_lru_w, row(conv_lru_b), w_rg_a.astype(BF16), row(b_rg_a),
                   w_rg_x.astype(BF16), row(b_rg_x), row(lru_lambda), row(g_grp_lru), bt_lru)

    y_attn = _dsa(proj, q_off, qi_off, k_off, small, row(g_grp_attn), d_attn, d_kv, bq, kb_sz)

    x1, h2 = _out_proj(y_lru, y_attn, w_out.astype(BF16), x, row(g_post_mix), gt1, row(g_pre_ffn), sc2, sh2,
                       bm_out, tn_out)

    d_ff = w_ffn_out.shape[0]
    pad = -d_ff % tf
    halves = lambda a: jnp.concatenate([jnp.pad(a[:, :d_ff], ((0, 0), (0, pad))),
                                        jnp.pad(a[:, d_ff:], ((0, 0), (0, pad)))], axis=1)
    return _conv_ffn(h2, halves(w_ffn_in).astype(BF16), halves(conv_ffn_w), halves(row(conv_ffn_b)),
                     jnp.pad(w_ffn_out, ((0, pad), (0, 0))).astype(BF16), x1, row(g_post_ffn), gt2, bm_ffn, tf)


_TILES = dict(bm_in=512, bn_in=1536, bt_lru=256, bq=256, kb_sz=512, bm_out=512, tn_out=512, bm_ffn=512, tf=256)


def kernel(x, c, w_ada, b_ada, g_pre_mix, g_post_mix, g_pre_ffn, g_post_ffn, w_in, conv_lru_w, conv_lru_b,
           w_rg_a, b_rg_a, w_rg_x, b_rg_x, lru_lambda, g_grp_lru, g_grp_attn, w_out, w_ffn_in,
           conv_ffn_w, conv_ffn_b, w_ffn_out):
    for l in range(w_ada.shape[0]):
        x = _layer(x, c, w_ada[l], b_ada[l], g_pre_mix[l], g_post_mix[l], g_pre_ffn[l], g_post_ffn[l],
                   w_in[l], conv_lru_w[l], conv_lru_b[l], w_rg_a[l], b_rg_a[l], w_rg_x[l], b_rg_x[l],
                   lru_lambda[l], g_grp_lru[l], g_grp_attn[l], w_out[l], w_ffn_in[l], conv_ffn_w[l],
                   conv_ffn_b[l], w_ffn_out[l], **_TILES)
    return x
```
